```python
import jax, jax.numpy as jnp
from jax import lax
import numpy as np

D_MODEL = 2048
BATCH = 4
SEQ = 2048
DEPTH = 1
DEC_BATCH = 128
DEC_SEQ = 1
PAST_LEN = 2048
PAGE_SIZE = 128

HEAD_DIM = 128
SB_HEADS = 8
MB_HEADS = 8
MEM_HEADS = 4
MEM_HEAD_DIM = 256
N_MEM = 256
SB_W = SB_HEADS * HEAD_DIM
MB_W = MB_HEADS * HEAD_DIM
MEM_W = MEM_HEADS * MEM_HEAD_DIM
MOBA_BLOCK = 256
MOBA_TOPK = 3
ROPE_THETA = 500000.0
ROPE_DIM = HEAD_DIM // 4
Q_BLOCK = 128
EPS = 1e-6
IN_SPLITS = (SB_W,) * 4 + (MB_W,) * 4 + (MEM_W,) * 2 + (D_MODEL,) * 3
IN_COLS = sum(IN_SPLITS)

kernel_name = 'hybrid_stickbreak_moba_memory_step'


def rmsnorm(x, g):
    xf = x.astype(jnp.float32)
    xf = xf * lax.rsqrt(jnp.mean(xf * xf, axis=-1, keepdims=True) + EPS)
    return xf.astype(x.dtype) * g


def rope_partial(x, pos):
    half = ROPE_DIM // 2
    inv = ROPE_THETA ** (-jnp.arange(half, dtype=jnp.float32) / half)
    ang = pos.astype(jnp.float32)[:, None] * inv[None, :]
    cos = jnp.cos(ang)[None, :, None, :]
    sin = jnp.sin(ang)[None, :, None, :]
    xr = x[..., :ROPE_DIM].astype(jnp.float32)
    x1, x2 = xr[..., :half], xr[..., half:]
    rot = jnp.concatenate([x1 * cos - x2 * sin, x2 * cos + x1 * sin], axis=-1)
    return jnp.concatenate([rot.astype(x.dtype), x[..., ROPE_DIM:]], axis=-1)


def in_proj(x, pos, norm_g, w_in, qn_moba, kn_moba, qn_mem):
    b, t, _ = x.shape
    h = rmsnorm(x, norm_g)
    proj = jnp.einsum('btd,de->bte', h, w_in)
    offs = [int(o) for o in np.cumsum(IN_SPLITS)[:-1]]
    qa, ka, va, ga, qb, kb, vb, gb, qm, gm, ma, mb, mm = jnp.split(proj, offs, axis=-1)
    qa = qa.reshape(b, t, SB_HEADS, HEAD_DIM)
    ka = ka.reshape(b, t, SB_HEADS, HEAD_DIM)
    va = va.reshape(b, t, SB_HEADS, HEAD_DIM)
    qb = rope_partial(rmsnorm(qb.reshape(b, t, MB_HEADS, HEAD_DIM), qn_moba), pos)
    kb = rope_partial(rmsnorm(kb.reshape(b, t, MB_HEADS, HEAD_DIM), kn_moba), pos)
    vb = vb.reshape(b, t, MB_HEADS, HEAD_DIM)
    qm = rmsnorm(qm.reshape(b, t, MEM_HEADS, MEM_HEAD_DIM), qn_mem)
    return (qa, ka, va, qb, kb, vb, qm), (ga, gb, gm), (ma, mb, mm)


def out_proj(x, outs, gates, merge_logits, w_br_sb, w_br_moba, w_br_mem, w_out):
    b, t, _ = x.shape
    merged = 0.0
    for o, g, m, w in zip(outs, gates, merge_logits, (w_br_sb, w_br_moba, w_br_mem)):
        u = jnp.einsum('bte,ed->btd', o.reshape(b, t, -1) * jax.nn.silu(g), w)
        merged = merged + jax.nn.sigmoid(m) * u
    return x + jnp.einsum('btd,de->bte', merged, w_out)


def sb_weights(z, q_pos, k_pos):
    mask = k_pos[None, :] < q_pos[:, None]
    l_neg = jnp.where(mask, jax.nn.log_sigmoid(-z), 0.0)
    after = lax.cumsum(l_neg, axis=z.ndim - 1, reverse=True) - l_neg
    return jnp.where(mask, jnp.exp(jax.nn.log_sigmoid(z) + after), 0.0)


def sb_prompt(q, k, v):
    b, s, h, d = q.shape
    k_pos = jnp.arange(s)
    scale = d ** -0.5

    def block(i):
        qb = lax.dynamic_slice_in_dim(q, i * Q_BLOCK, Q_BLOCK, axis=1)
        q_pos = i * Q_BLOCK + jnp.arange(Q_BLOCK)
        z = jnp.einsum('bqhd,bkhd->bhqk', qb, k, preferred_element_type=jnp.float32) * scale
        a = sb_weights(z, q_pos, k_pos).astype(v.dtype)
        return jnp.einsum('bhqk,bkhd->bqhd', a, v)

    out = lax.map(block, jnp.arange(s // Q_BLOCK))
    return jnp.moveaxis(out, 0, 1).reshape(b, s, h, d)


def sb_sample(q, k_past, v_past, k_new, v_new, q_pos):
    p = k_past.shape[1]
    tn = k_new.shape[1]
    scale = q.shape[-1] ** -0.5
    z = jnp.concatenate([
        jnp.einsum('bqhd,bkhd->bhqk', q, k_past, preferred_element_type=jnp.float32),
        jnp.einsum('bqhd,bkhd->bhqk', q, k_new, preferred_element_type=jnp.float32)], axis=-1) * scale
    a = sb_weights(z, q_pos, jnp.arange(p + tn)).astype(v_new.dtype)
    return (jnp.einsum('bhqk,bkhd->bqhd', a[..., :p], v_past)
            + jnp.einsum('bhqk,bkhd->bqhd', a[..., p:], v_new))


def moba_blocks(k_segs, v_segs):
    b, _, h, d = k_segs[0].shape
    l = sum(s.shape[1] for s in k_segs)
    lp = -(-l // MOBA_BLOCK) * MOBA_BLOCK
    pad = jnp.zeros((b, lp - l, h, d), k_segs[0].dtype)
    kb = jnp.concatenate(list(k_segs) + [pad], axis=1).reshape(b, lp // MOBA_BLOCK, MOBA_BLOCK, h, d)
    vb = jnp.concatenate(list(v_segs) + [pad.astype(v_segs[0].dtype)], axis=1).reshape(b, lp // MOBA_BLOCK, MOBA_BLOCK, h, d)
    km = jnp.mean(kb.astype(jnp.float32), axis=2)
    return kb, vb, km


def moba_attend(q, q_pos, kb, vb, km):
    b, tq, h, d = q.shape
    nb = kb.shape[1]
    scale = d ** -0.5
    own = q_pos // MOBA_BLOCK
    g = jnp.einsum('bqhd,bnhd->bhqn', q.astype(jnp.float32), km)
    g = jnp.where(jnp.arange(nb)[None, :] < own[:, None], g, -jnp.inf)
    k_sel = min(MOBA_TOPK, nb)
    _, top = lax.top_k(g, k_sel)
    top_valid = jnp.arange(k_sel)[None, :] < own[:, None]
    idx_list = [top[..., r] for r in range(k_sel)] + [jnp.broadcast_to(own[None, None, :], (b, h, tq))]
    bi = jnp.arange(b)[:, None, None]
    hi = jnp.arange(h)[None, :, None]
    qt = jnp.transpose(q, (0, 2, 1, 3))
    scores = []
    for idx in idx_list:
        k_g = kb[bi, idx, :, hi, :]
        scores.append(jnp.einsum('bhqd,bhqkd->bhqk', qt, k_g, preferred_element_type=jnp.float32) * scale)
    j = jnp.arange(MOBA_BLOCK)
    own_mask = (own[:, None] * MOBA_BLOCK + j[None, :]) <= q_pos[:, None]
    mask = jnp.concatenate([jnp.repeat(top_valid, MOBA_BLOCK, axis=1), own_mask], axis=1)
    s_all = jnp.where(mask, jnp.concatenate(scores, axis=-1), -jnp.inf)
    p = jax.nn.softmax(s_all, axis=-1).astype(vb.dtype)
    out = 0.0
    for r, idx in enumerate(idx_list):
        v_g = vb[bi, idx, :, hi, :]
        out = out + jnp.einsum('bhqk,bhqkd->bqhd', p[..., r * MOBA_BLOCK:(r + 1) * MOBA_BLOCK], v_g)
    return out


def moba_prompt(q, k, v):
    b, s, h, d = q.shape
    kb, vb, km = moba_blocks([k], [v])

    def block(i):
        qb = lax.dynamic_slice_in_dim(q, i * Q_BLOCK, Q_BLOCK, axis=1)
        return moba_attend(qb, i * Q_BLOCK + jnp.arange(Q_BLOCK), kb, vb, km)

    out = lax.map(block, jnp.arange(s // Q_BLOCK))
    return jnp.moveaxis(out, 0, 1).reshape(b, s, h, d)


def mem_kv(mem, mem_norm_g, w_mem_kv, kn_mem):
    b, n, _ = mem.shape
    kv = jnp.einsum('bnd,de->bne', rmsnorm(mem, mem_norm_g), w_mem_kv)
    k, v = jnp.split(kv, 2, axis=-1)
    k = rmsnorm(k.reshape(b, n, MEM_HEADS, MEM_HEAD_DIM), kn_mem)
    return k, v.reshape(b, n, MEM_HEADS, MEM_HEAD_DIM)


def mem_attend(q, mk, mv):
    s = jnp.einsum('bqhd,bkhd->bhqk', q, mk, preferred_element_type=jnp.float32) * (q.shape[-1] ** -0.5)
    p = jax.nn.softmax(s, axis=-1).astype(mv.dtype)
    return jnp.einsum('bhqk,bkhd->bqhd', p, mv)


def gather_pages(pool, page_table):
    g = pool[page_table]
    return g.reshape(g.shape[0], g.shape[1] * g.shape[2], *g.shape[3:])


def setup_inputs(seed: int = 0) -> dict:
    key = jax.random.key(seed)
    ks = jax.random.split(key, 24)
    n_pages = PAST_LEN // PAGE_SIZE
    n_used = DEC_BATCH * n_pages
    n_pool = n_used + n_used // 4
    nrm = lambda k, shape, s=1.0: s * jax.random.normal(k, shape, jnp.float32)
    gain = lambda k, n: 1.0 + 0.02 * jax.random.normal(k, (n,), jnp.float32)
    page_table = jax.random.permutation(ks[0], n_pool)[:n_used].reshape(DEC_BATCH, n_pages).astype(jnp.int32)
    return {
        'x_prompt': nrm(ks[1], (BATCH, SEQ, D_MODEL)),
        'x_sample': nrm(ks[2], (DEC_BATCH, DEC_SEQ, D_MODEL)),
        'mem_prompt': nrm(ks[3], (BATCH, N_MEM, D_MODEL)),
        'cache_sb_k': nrm(ks[4], (n_pool, PAGE_SIZE, SB_HEADS, HEAD_DIM)),
        'cache_sb_v': nrm(ks[5], (n_pool, PAGE_SIZE, SB_HEADS, HEAD_DIM)),
        'cache_moba_k': nrm(ks[6], (n_pool, PAGE_SIZE, MB_HEADS, HEAD_DIM)),
        'cache_moba_v': nrm(ks[7], (n_pool, PAGE_SIZE, MB_HEADS, HEAD_DIM)),
        'cache_mem_k': nrm(ks[8], (DEC_BATCH, N_MEM, MEM_HEADS, MEM_HEAD_DIM)),
        'cache_mem_v': nrm(ks[9], (DEC_BATCH, N_MEM, MEM_HEADS, MEM_HEAD_DIM)),
        'page_table': page_table,
        'norm_g': gain(ks[10], D_MODEL),
        'w_in': nrm(ks[11], (D_MODEL, IN_COLS), D_MODEL ** -0.5),
        'qn_moba': gain(ks[12], HEAD_DIM),
        'kn_moba': gain(ks[13], HEAD_DIM),
        'qn_mem': gain(ks[14], MEM_HEAD_DIM),
        'kn_mem': gain(ks[15], MEM_HEAD_DIM),
        'mem_norm_g': gain(ks[16], D_MODEL),
        'w_mem_kv': nrm(ks[17], (D_MODEL, 2 * MEM_W), D_MODEL ** -0.5),
        'w_br_sb': nrm(ks[18], (SB_W, D_MODEL), SB_W ** -0.5),
        'w_br_moba': nrm(ks[19], (MB_W, D_MODEL), MB_W ** -0.5),
        'w_br_mem': nrm(ks[20], (MEM_W, D_MODEL), MEM_W ** -0.5),
        'w_out': nrm(ks[21], (D_MODEL, D_MODEL), D_MODEL ** -0.5),
    }


def reference(x_prompt, x_sample, mem_prompt, cache_sb_k, cache_sb_v, cache_moba_k, cache_moba_v,
              cache_mem_k, cache_mem_v, page_table, norm_g, w_in, qn_moba, kn_moba, qn_mem, kn_mem,
              mem_norm_g, w_mem_kv, w_br_sb, w_br_moba, w_br_mem, w_out):
    p_pos = jnp.arange(x_prompt.shape[1])
    (qa, ka, va, qb, kb, vb, qm), gates, merges = in_proj(x_prompt, p_pos, norm_g, w_in, qn_moba, kn_moba, qn_mem)
    oa = sb_prompt(qa, ka, va)
    ob = moba_prompt(qb, kb, vb)
    mk_p, mv_p = mem_kv(mem_prompt, mem_norm_g, w_mem_kv, kn_mem)
    om = mem_attend(qm, mk_p, mv_p)
    y_prompt = out_proj(x_prompt, (oa, ob, om), gates, merges, w_br_sb, w_br_moba, w_br_mem, w_out)

    past_len = page_table.shape[1] * cache_sb_k.shape[1]
    s_pos = past_len + jnp.arange(x_sample.shape[1])
    (qa_s, ka_s, va_s, qb_s, kb_s, vb_s, qm_s), gates_s, merges_s = in_proj(
        x_sample, s_pos, norm_g, w_in, qn_moba, kn_moba, qn_mem)
    oa_s = sb_sample(qa_s, gather_pages(cache_sb_k, page_table), gather_pages(cache_sb_v, page_table),
                     ka_s, va_s, s_pos)
    kblk, vblk, kmean = moba_blocks([gather_pages(cache_moba_k, page_table), kb_s],
                                    [gather_pages(cache_moba_v, page_table), vb_s])
    ob_s = moba_attend(qb_s, s_pos, kblk, vblk, kmean)
    om_s = mem_attend(qm_s, cache_mem_k, cache_mem_v)
    y_sample = out_proj(x_sample, (oa_s, ob_s, om_s), gates_s, merges_s, w_br_sb, w_br_moba, w_br_mem, w_out)

    return (y_prompt, y_sample, ka, va, kb, vb, mk_p, mv_p, ka_s, va_s, kb_s, vb_s)
```

```python
import functools
import math

import jax
import jax.numpy as jnp
from jax import lax
from jax.experimental import pallas as pl
from jax.experimental.pallas import tpu as pltpu

F32 = jnp.float32
BF16 = jnp.bfloat16

HEAD_DIM = 128
SB_HEADS = 8
MB_HEADS = 8
MEM_HEADS = 4
MEM_HEAD_DIM = 256
MOBA_BLOCK = 256
MOBA_TOPK = 3
ROPE_THETA = 500000.0
ROPE_DIM = HEAD_DIM // 4
EPS = 1e-6

LANES = 128
NEG = -1e30
VMEM_LIMIT = 56 * 1024 * 1024

ID, SILU, SIGMOID, HNORM_ROPE, HNORM = range(5)


def _nt_dot(a, b):
    return lax.dot_general(a, b, (((1,), (1,)), ((), ())), preferred_element_type=F32)


def _dot(a, b):
    return jnp.dot(a, b, preferred_element_type=F32)


def _sigmoid(x):
    return 1.0 / (1.0 + jnp.exp(-x))


def _split_bf16(x, parts):
    out = []
    for _ in range(parts - 1):
        hi = x.astype(BF16)
        out.append(hi)
        x = x - hi.astype(F32)
    out.append(x.astype(BF16))
    return out


def _log_sigmoids(z):
    sp = jnp.log1p(jnp.exp(-jnp.abs(z)))
    return jnp.minimum(z, 0.0) - sp, jnp.minimum(-z, 0.0) - sp


def _rope_tables(pos, half, theta):
    rows = pos.shape[0]
    lane = lax.broadcasted_iota(jnp.int32, (rows, LANES), 1)
    freq = (lane % half).astype(F32) * (1.0 / half)
    ang = pos.astype(F32) * jnp.exp(freq * (-math.log(theta)))
    in_rope = lane < 2 * half
    cos = jnp.where(in_rope, jnp.cos(ang), 1.0)
    sin = jnp.where(in_rope, jnp.sin(ang), 0.0)
    s_lo = jnp.where(lane < half, -sin, 0.0)
    s_hi = jnp.where(lane >= half, sin, 0.0)
    return cos, s_lo, s_hi


def _proj_kernel(x_ref, g_ref, w_ref, gains_ref, o_ref, h_ref, *, kinds, tm, seq, pos0):
    i = pl.program_id(0)
    j = pl.program_id(1)

    @pl.when(j == 0)
    def _():
        x = x_ref[...]
        ms = jnp.mean(x * x, axis=-1, keepdims=True)
        h_ref[...] = ((x * lax.rsqrt(ms + EPS)) * g_ref[...]).astype(BF16)

    def tiles_of(kind_pred):
        cond = None
        for c, kd in enumerate(kinds):
            if kind_pred(kd):
                t = j == c
                cond = t if cond is None else jnp.logical_or(cond, t)
        return cond

    def matmul():
        return _dot(h_ref[...], w_ref[...])

    cond = tiles_of(lambda kd: kd[0] == ID)
    if cond is not None:
        @pl.when(cond)
        def _():
            o_ref[...] = matmul()

    cond = tiles_of(lambda kd: kd[0] == SILU)
    if cond is not None:
        @pl.when(cond)
        def _():
            a = matmul()
            o_ref[...] = a * _sigmoid(a)

    cond = tiles_of(lambda kd: kd[0] == SIGMOID)
    if cond is not None:
        @pl.when(cond)
        def _():
            o_ref[...] = _sigmoid(matmul())

    for c, kd in enumerate(kinds):
        if kd[0] not in (HNORM_ROPE, HNORM):
            continue

        @pl.when(j == c)
        def _(kd=kd):
            a = matmul()
            hd = kd[1]
            gain = gains_ref[kd[2]:kd[2] + 1, :hd]
            if kd[0] == HNORM_ROPE:
                row = lax.broadcasted_iota(jnp.int32, (tm, 1), 0) + i * tm
                pos = pos0 + lax.rem(row, seq)
                cos, s_lo, s_hi = _rope_tables(pos, ROPE_DIM // 2, ROPE_THETA)
            for h in range(a.shape[1] // hd):
                ah = a[:, h * hd:(h + 1) * hd]
                ms = jnp.mean(ah * ah, axis=-1, keepdims=True)
                ah = (ah * lax.rsqrt(ms + EPS)) * gain
                if kd[0] == HNORM_ROPE:
                    half = ROPE_DIM // 2
                    ah = (ah * cos + pltpu.roll(ah, LANES - half, 1) * s_lo
                          + pltpu.roll(ah, half, 1) * s_hi)
                o_ref[:, h * hd:(h + 1) * hd] = ah


def _normed_proj(x, norm_g, w_bf16, gains, kinds, *, tm, tn, seq, pos0):
    r, d = x.shape
    n = w_bf16.shape[1]
    assert r % tm == 0 and n % tn == 0 and len(kinds) == n // tn
    return pl.pallas_call(
        functools.partial(_proj_kernel, kinds=kinds, tm=tm, seq=seq, pos0=pos0),
        grid=(r // tm, n // tn),
        in_specs=[
            pl.BlockSpec((tm, d), lambda i, j: (i, 0)),
            pl.BlockSpec((1, d), lambda i, j: (0, 0)),
            pl.BlockSpec((d, tn), lambda i, j: (0, j)),
            pl.BlockSpec(gains.shape, lambda i, j: (0, 0)),
        ],
        out_specs=pl.BlockSpec((tm, tn), lambda i, j: (i, j)),
        out_shape=jax.ShapeDtypeStruct((r, n), F32),
        scratch_shapes=[pltpu.VMEM((tm, d), BF16)],
        compiler_params=pltpu.CompilerParams(
            dimension_semantics=("arbitrary", "arbitrary"), vmem_limit_bytes=VMEM_LIMIT),
        name="normed_proj",
    )(x, norm_g.reshape(1, d), w_bf16, gains)


def _sb_prompt_kernel(q_ref, k_ref, v_ref, o_ref, *, tq, scale):
    qi = pl.program_id(2)
    d = q_ref.shape[1]
    q = q_ref[...].astype(BF16)
    row = lax.broadcasted_iota(jnp.int32, (tq, tq), 0)
    col = lax.broadcasted_iota(jnp.int32, (tq, tq), 1)
    later = (row > col).astype(BF16)

    def body(it, carry):
        c, acc = carry
        j = qi - it
        start = pl.multiple_of(j * tq, tq)
        kb = k_ref[pl.ds(start, tq), :].astype(BF16)
        vb = v_ref[pl.ds(start, tq), :].astype(BF16)
        z = _nt_dot(q, kb) * scale
        mask = (col + j * tq) < (row + qi * tq)
        ls_pos, ls_neg = _log_sigmoids(z)
        l_neg = jnp.where(mask, ls_neg, 0.0)
        l_hi, l_lo = _split_bf16(l_neg, 2)
        after = _dot(l_hi, later) + _dot(l_lo, later) + c
        a = jnp.where(mask, jnp.exp(ls_pos + after), 0.0)
        acc = acc + _dot(a.astype(BF16), vb)
        c = c + jnp.sum(l_neg, axis=1, keepdims=True)
        return c, acc

    _, acc = lax.fori_loop(0, qi + 1, body,
                           (jnp.zeros((tq, 1), F32), jnp.zeros((tq, d), F32)))
    o_ref[...] = acc


def _sb_prompt(proj, *, batch, seq, heads, q_col, k_col, v_col, tq):
    d = HEAD_DIM
    nq = seq // tq
    return pl.pallas_call(
        functools.partial(_sb_prompt_kernel, tq=tq, scale=d ** -0.5),
        grid=(batch, heads, nq),
        in_specs=[
            pl.BlockSpec((tq, d), lambda b, h, i: (b * nq + i, q_col // d + h)),
            pl.BlockSpec((seq, d), lambda b, h, i: (b, k_col // d + h)),
            pl.BlockSpec((seq, d), lambda b, h, i: (b, v_col // d + h)),
        ],
        out_specs=pl.BlockSpec((tq, d), lambda b, h, i: (b * nq + i, h)),
        out_shape=jax.ShapeDtypeStruct((batch * seq, heads * d), F32),
        compiler_params=pltpu.CompilerParams(
            dimension_semantics=("arbitrary", "arbitrary", "arbitrary"),
            vmem_limit_bytes=VMEM_LIMIT),
        name="sb_prompt",
    )(proj, proj, proj)


def _topk_select(g, valid, topk):
    nb = len(g)
    sel = []
    for m in range(nb):
        rank = jnp.zeros_like(g[m])
        for o in range(nb):
            if o == m:
                continue
            beats = g[o] >= g[m] if o < m else g[o] > g[m]
            rank = rank + jnp.where(beats, valid[o], 0.0)
        sel.append(jnp.where(rank < topk, valid[m], 0.0))
    return sel


def _moba_prompt_kernel(q_ref, k_ref, v_ref, o_ref, km_ref, sel_ref, *, blk, nb, topk, scale):
    qi = pl.program_id(2)

    @pl.when(qi == 0)
    def _():
        for n in range(nb):
            km_ref[n:n + 1, :] = jnp.mean(k_ref[n * blk:(n + 1) * blk, :], axis=0, keepdims=True)

    q = q_ref[...].astype(BF16)
    g = _nt_dot(km_ref[...].astype(BF16), q)
    valid = [(n < qi).astype(F32) for n in range(nb)]
    sel_t = jnp.concatenate(
        _topk_select([g[n:n + 1, :] for n in range(nb)], valid, topk), axis=0)
    row = lax.broadcasted_iota(jnp.int32, (blk, blk), 0)
    col = lax.broadcasted_iota(jnp.int32, (blk, blk), 1)
    eye = (row == col).astype(BF16)
    sel_c = _nt_dot(eye, sel_t.astype(BF16))
    for n in range(nb):
        sel_ref[n] = jnp.broadcast_to(sel_c[:, n:n + 1], (blk, LANES))

    start = pl.multiple_of(qi * blk, blk)
    s = _nt_dot(q, k_ref[pl.ds(start, blk), :].astype(BF16)) * scale
    s = jnp.where(col <= row, s, NEG)
    m = jnp.max(s, axis=1, keepdims=True)
    p = jnp.exp(s - m)
    l = jnp.sum(p, axis=1, keepdims=True)
    acc = _dot(p.astype(BF16), v_ref[pl.ds(start, blk), :].astype(BF16))

    def body(n, carry):
        m, l, acc = carry
        start = pl.multiple_of(n * blk, blk)
        s = _nt_dot(q, k_ref[pl.ds(start, blk), :].astype(BF16)) * scale
        picked = jnp.concatenate([sel_ref[n]] * (blk // LANES), axis=1) > 0.5
        s = jnp.where(picked, s, NEG)
        m_new = jnp.maximum(m, jnp.max(s, axis=1, keepdims=True))
        alpha = jnp.exp(m - m_new)
        p = jnp.exp(s - m_new)
        l = alpha * l + jnp.sum(p, axis=1, keepdims=True)
        acc = alpha * acc + _dot(p.astype(BF16), v_ref[pl.ds(start, blk), :].astype(BF16))
        return m_new, l, acc

    m, l, acc = lax.fori_loop(0, qi, body, (m, l, acc))
    o_ref[...] = acc / l


def _moba_prompt(proj, *, batch, seq, heads, q_col, k_col, v_col):
    d = HEAD_DIM
    blk = MOBA_BLOCK
    nb = seq // blk
    return pl.pallas_call(
        functools.partial(_moba_prompt_kernel, blk=blk, nb=nb, topk=MOBA_TOPK, scale=d ** -0.5),
        grid=(batch, heads, nb),
        in_specs=[
            pl.BlockSpec((blk, d), lambda b, h, i: (b * nb + i, q_col // d + h)),
            pl.BlockSpec((seq, d), lambda b, h, i: (b, k_col // d + h)),
            pl.BlockSpec((seq, d), lambda b, h, i: (b, v_col // d + h)),
        ],
        out_specs=pl.BlockSpec((blk, d), lambda b, h, i: (b * nb + i, h)),
        out_shape=jax.ShapeDtypeStruct((batch * seq, heads * d), F32),
        scratch_shapes=[pltpu.VMEM((nb, d), F32), pltpu.VMEM((nb, blk, LANES), F32)],
        compiler_params=pltpu.CompilerParams(
            dimension_semantics=("arbitrary", "arbitrary", "arbitrary"),
            vmem_limit_bytes=VMEM_LIMIT),
        name="moba_prompt",
    )(proj, proj, proj)


def _mem_prompt_kernel(q_ref, k_ref, v_ref, o_ref, *, scale):
    s = _nt_dot(q_ref[...].astype(BF16), k_ref[...].astype(BF16)) * scale
    m = jnp.max(s, axis=1, keepdims=True)
    p = jnp.exp(s - m)
    l = jnp.sum(p, axis=1, keepdims=True)
    o_ref[...] = _dot(p.astype(BF16), v_ref[...].astype(BF16)) / l


def _mem_prompt(proj, memkv, *, batch, seq, n_mem, heads, q_col, tq):
    d = MEM_HEAD_DIM
    nq = seq // tq
    return pl.pallas_call(
        functools.partial(_mem_prompt_kernel, scale=d ** -0.5),
        grid=(batch, heads, nq),
        in_specs=[
            pl.BlockSpec((tq, d), lambda b, h, i: (b * nq + i, q_col // d + h)),
            pl.BlockSpec((n_mem, d), lambda b, h, i: (b, h)),
            pl.BlockSpec((n_mem, d), lambda b, h, i: (b, heads + h)),
        ],
        out_specs=pl.BlockSpec((tq, d), lambda b, h, i: (b * nq + i, h)),
        out_shape=jax.ShapeDtypeStruct((batch * seq, heads * d), F32),
        compiler_params=pltpu.CompilerParams(
            dimension_semantics=("arbitrary", "arbitrary", "arbitrary"),
            vmem_limit_bytes=VMEM_LIMIT),
        name="mem_prompt",
    )(proj, memkv, memkv)


def _diag_mask(nh, width):
    sub = lax.broadcasted_iota(jnp.int32, (nh, width), 0)
    lane = lax.broadcasted_iota(jnp.int32, (nh, width), 1)
    return (lane % nh) == sub


def _class_scan(x, nh, cyclic):
    width = x.shape[-1]
    lane = lax.broadcasted_iota(jnp.int32, x.shape, x.ndim - 1)
    shift = nh
    while shift < width:
        nxt = pltpu.roll(x, width - shift, x.ndim - 1)
        x = x + (nxt if cyclic else jnp.where(lane + shift < width, nxt, 0.0))
        shift *= 2
    return x


def _page_specs(n_pages, rows, d):
    def spec(p):
        return pl.BlockSpec((1, rows, d), lambda b, pt: (pt[b, p], 0, 0))
    return [spec(p) for p in range(n_pages)]


def _sb_sample_kernel(pt_ref, q_ref, kn_ref, vn_ref, *refs, n_pages, scale, past_len):
    del pt_ref
    k_refs, v_refs = refs[:n_pages], refs[n_pages:2 * n_pages]
    o_ref, z_ref = refs[2 * n_pages:]
    nh, d = q_ref.shape[1:]
    width = k_refs[0].shape[1]
    slots = width // nh
    q = q_ref[0].astype(BF16)
    diag = _diag_mask(nh, width)

    def diag_row(scores):
        return jnp.sum(jnp.where(diag[:, :scores.shape[1]], scores, 0.0), axis=0, keepdims=True)

    for p in range(n_pages):
        z_ref[p:p + 1, :] = diag_row(_nt_dot(q, k_refs[p][0].astype(BF16)))
    z = z_ref[...] * scale

    lane = lax.broadcasted_iota(jnp.int32, (1, width), 1)
    prow = lax.broadcasted_iota(jnp.int32, (n_pages, width), 0)
    plane = lax.broadcasted_iota(jnp.int32, (n_pages, width), 1)
    q_pos = past_len

    kn = jnp.concatenate([kn_ref[0], jnp.zeros((LANES - nh, d), F32)], axis=0).astype(BF16)
    vn = jnp.concatenate([vn_ref[0], jnp.zeros((LANES - nh, d), F32)], axis=0).astype(BF16)
    z_n = jnp.concatenate([diag_row(_nt_dot(q, kn)), jnp.zeros((1, width - LANES), F32)],
                          axis=1) * scale
    mask_n = (past_len + lane // nh) < q_pos
    lsp_n, lsn_n = _log_sigmoids(z_n)
    l_neg_n = jnp.where(mask_n, lsn_n, 0.0)
    a_n = jnp.where(mask_n, jnp.exp(lsp_n), 0.0)
    total_n = _class_scan(l_neg_n, nh, cyclic=True)

    mask = (prow * slots + plane // nh) < q_pos
    ls_pos, ls_neg = _log_sigmoids(z)
    l_neg = jnp.where(mask, ls_neg, 0.0)
    within = _class_scan(l_neg, nh, cyclic=False) - l_neg
    totals = _class_scan(l_neg, nh, cyclic=True)
    pr = lax.broadcasted_iota(jnp.int32, (n_pages, n_pages), 0)
    pc = lax.broadcasted_iota(jnp.int32, (n_pages, n_pages), 1)
    later = (pc > pr).astype(BF16)
    carry = sum(_dot(later, part) for part in _split_bf16(totals, 3))
    a = jnp.where(mask, jnp.exp(ls_pos + within + carry + total_n), 0.0)

    def spread(w_row):
        w = w_row.shape[1]
        return jnp.where(diag[:, :w], jnp.broadcast_to(w_row, (nh, w)), 0.0).astype(BF16)

    acc = _dot(spread(a_n[:, :LANES]), vn)
    for p in range(n_pages):
        acc = acc + _dot(spread(a[p:p + 1, :]), v_refs[p][0].astype(BF16))
    o_ref[0] = acc


def _sb_sample(page_table, q, k_new, v_new, k_pool, v_pool):
    b, nh, d = q.shape
    n_pages = page_table.shape[1]
    rows = k_pool.shape[1]
    row_spec = pl.BlockSpec((1, nh, d), lambda i, pt: (i, 0, 0))
    pages = _page_specs(n_pages, rows, d)
    grid_spec = pltpu.PrefetchScalarGridSpec(
        num_scalar_prefetch=1,
        grid=(b,),
        in_specs=[row_spec, row_spec, row_spec] + pages + pages,
        out_specs=row_spec,
        scratch_shapes=[pltpu.VMEM((n_pages, rows), F32)],
    )
    return pl.pallas_call(
        functools.partial(_sb_sample_kernel, n_pages=n_pages, scale=d ** -0.5,
                          past_len=n_pages * (rows // nh)),
        grid_spec=grid_spec,
        out_shape=jax.ShapeDtypeStruct((b, nh, d), F32),
        compiler_params=pltpu.CompilerParams(
            dimension_semantics=("arbitrary",), vmem_limit_bytes=VMEM_LIMIT),
        name="sb_sample",
    )(page_table, q, k_new, v_new, *([k_pool] * n_pages), *([v_pool] * n_pages))


def _moba_sample_kernel(pt_ref, q_ref, kn_ref, vn_ref, *refs, n_pages, ppb, topk, scale):
    del pt_ref
    k_refs, v_refs = refs[:n_pages], refs[n_pages:2 * n_pages]
    o_ref, s_ref = refs[2 * n_pages:]
    nh, d = q_ref.shape[1:]
    width = k_refs[0].shape[1]
    slots = width // nh
    nb = n_pages // ppb
    q = q_ref[0].astype(BF16)
    qf = q.astype(F32)
    diag = _diag_mask(nh, width)

    km = []
    for p in range(n_pages):
        kp = k_refs[p][0]
        s_ref[p] = _nt_dot(q, kp.astype(BF16)) * scale
        ksum = jnp.sum(kp.reshape(slots, nh, d), axis=0)
        if p % ppb == 0:
            km.append(ksum)
        else:
            km[-1] = km[-1] + ksum
    g = [jnp.sum(qf * (kb * (1.0 / (ppb * slots))).astype(BF16).astype(F32),
                 axis=1, keepdims=True) for kb in km]
    sel = _topk_select(g, [1.0] * nb, topk)

    s_new = jnp.sum(qf * kn_ref[0].astype(BF16).astype(F32), axis=1, keepdims=True) * scale
    m = s_new
    for p in range(n_pages):
        sp = jnp.where(jnp.logical_and(diag, sel[p // ppb] > 0.5), s_ref[p], NEG)
        s_ref[p] = sp
        m = jnp.maximum(m, jnp.max(sp, axis=1, keepdims=True))
    w_new = jnp.exp(s_new - m)
    l = w_new
    acc = w_new * vn_ref[0]
    for p in range(n_pages):
        pp = jnp.exp(s_ref[p] - m)
        l = l + jnp.sum(pp, axis=1, keepdims=True)
        acc = acc + _dot(pp.astype(BF16), v_refs[p][0].astype(BF16))
    o_ref[0] = acc / l


def _moba_sample(page_table, q, k_new, v_new, k_pool, v_pool):
    b, nh, d = q.shape
    n_pages = page_table.shape[1]
    rows = k_pool.shape[1]
    slots = rows // nh
    ppb = MOBA_BLOCK // slots
    assert MOBA_BLOCK % slots == 0 and n_pages % ppb == 0
    row_spec = pl.BlockSpec((1, nh, d), lambda i, pt: (i, 0, 0))
    pages = _page_specs(n_pages, rows, d)
    grid_spec = pltpu.PrefetchScalarGridSpec(
        num_scalar_prefetch=1,
        grid=(b,),
        in_specs=[row_spec, row_spec, row_spec] + pages + pages,
        out_specs=row_spec,
        scratch_shapes=[pltpu.VMEM((n_pages, nh, rows), F32)],
    )
    return pl.pallas_call(
        functools.partial(_moba_sample_kernel, n_pages=n_pages, ppb=ppb, topk=MOBA_TOPK,
                          scale=d ** -0.5),
        grid_spec=grid_spec,
        out_shape=jax.ShapeDtypeStruct((b, nh, d), F32),
        compiler_params=pltpu.CompilerParams(
            dimension_semantics=("arbitrary",), vmem_limit_bytes=VMEM_LIMIT),
        name="moba_sample",
    )(page_table, q, k_new, v_new, *([k_pool] * n_pages), *([v_pool] * n_pages))


def _mem_sample_kernel(q_ref, k_ref, v_ref, o_ref, *, scale):
    nh = q_ref.shape[1]
    width = k_ref.shape[1]
    s = _nt_dot(q_ref[0].astype(BF16), k_ref[0].astype(BF16)) * scale
    s = jnp.where(_diag_mask(nh, width), s, NEG)
    m = jnp.max(s, axis=1, keepdims=True)
    p = jnp.exp(s - m)
    l = jnp.sum(p, axis=1, keepdims=True)
    o_ref[0] = _dot(p.astype(BF16), v_ref[0].astype(BF16)) / l


def _mem_sample(q, k_cache, v_cache):
    b, nh, d = q.shape
    rows = k_cache.shape[1]
    row_spec = pl.BlockSpec((1, nh, d), lambda i: (i, 0, 0))
    cache_spec = pl.BlockSpec((1, rows, d), lambda i: (i, 0, 0))
    return pl.pallas_call(
        functools.partial(_mem_sample_kernel, scale=d ** -0.5),
        grid=(b,),
        in_specs=[row_spec, cache_spec, cache_spec],
        out_specs=row_spec,
        out_shape=jax.ShapeDtypeStruct((b, nh, d), F32),
        compiler_params=pltpu.CompilerParams(
            dimension_semantics=("arbitrary",), vmem_limit_bytes=VMEM_LIMIT),
        name="mem_sample",
    )(q, k_cache, v_cache)


def _merge_kernel(oa_ref, ob_ref, om_ref, ga_ref, gb_ref, gm_ref, sa_ref, sb_ref, sm_ref,
                  wa_ref, wb_ref, wm_ref, o_ref):
    def branch(o_r, g_r, s_r, w_r):
        return s_r[...] * _dot((o_r[...] * g_r[...]).astype(BF16), w_r[...])
    merged = branch(oa_ref, ga_ref, sa_ref, wa_ref)
    merged = merged + branch(ob_ref, gb_ref, sb_ref, wb_ref)
    merged = merged + branch(om_ref, gm_ref, sm_ref, wm_ref)
    o_ref[...] = merged.astype(BF16)


def _merge(oa, ob, om, proj, wa, wb, wm, *, gate_cols, merge_cols, tm, tn):
    r, e = oa.shape
    d = wa.shape[1]
    assert r % tm == 0 and d % tn == 0
    act = pl.BlockSpec((tm, e), lambda i, n: (i, 0))
    gates = [pl.BlockSpec((tm, e), functools.partial(lambda i, n, c: (i, c), c=c // e))
             for c in gate_cols]
    merges = [pl.BlockSpec((tm, tn), functools.partial(lambda i, n, c: (i, c + n), c=c // tn))
              for c in merge_cols]
    wspec = pl.BlockSpec((e, tn), lambda i, n: (0, n))
    return pl.pallas_call(
        _merge_kernel,
        grid=(r // tm, d // tn),
        in_specs=[act, act, act] + gates + merges + [wspec, wspec, wspec],
        out_specs=pl.BlockSpec((tm, tn), lambda i, n: (i, n)),
        out_shape=jax.ShapeDtypeStruct((r, d), BF16),
        compiler_params=pltpu.CompilerParams(
            dimension_semantics=("arbitrary", "arbitrary"), vmem_limit_bytes=VMEM_LIMIT),
        name="merge",
    )(oa, ob, om, proj, proj, proj, proj, proj, proj, wa, wb, wm)


def _out_kernel(m_ref, w_ref, x_ref, o_ref):
    o_ref[...] = x_ref[...] + _dot(m_ref[...], w_ref[...])


def _out_proj(merged, w_out, x, *, tm, tn):
    r, d = merged.shape
    n = w_out.shape[1]
    assert r % tm == 0 and n % tn == 0
    return pl.pallas_call(
        _out_kernel,
        grid=(r // tm, n // tn),
        in_specs=[
            pl.BlockSpec((tm, d), lambda i, j: (i, 0)),
            pl.BlockSpec((d, tn), lambda i, j: (0, j)),
            pl.BlockSpec((tm, tn), lambda i, j: (i, j)),
        ],
        out_specs=pl.BlockSpec((tm, tn), lambda i, j: (i, j)),
        out_shape=jax.ShapeDtypeStruct((r, n), F32),
        compiler_params=pltpu.CompilerParams(
            dimension_semantics=("arbitrary", "arbitrary"), vmem_limit_bytes=VMEM_LIMIT),
        name="out_proj",
    )(merged, w_out, x)


def _layer(x, proj_fn, attend_fn, weights):
    wa, wb, wm, w_out, cols, tm = weights
    proj = proj_fn(x)
    oa, ob, om = attend_fn(proj)
    merged = _merge(oa, ob, om, proj, wa, wb, wm, gate_cols=cols["gates"],
                    merge_cols=cols["merges"], tm=tm, tn=1024)
    return proj, _out_proj(merged, w_out, x, tm=tm, tn=1024)


def kernel(x_prompt, x_sample, mem_prompt, cache_sb_k, cache_sb_v, cache_moba_k, cache_moba_v,
           cache_mem_k, cache_mem_v, page_table, norm_g, w_in, qn_moba, kn_moba, qn_mem, kn_mem,
           mem_norm_g, w_mem_kv, w_br_sb, w_br_moba, w_br_mem, w_out):
    batch, seq, d_model = x_prompt.shape
    dec_batch, dec_seq, _ = x_sample.shape
    n_mem = mem_prompt.shape[1]
    n_pool, page = cache_sb_k.shape[:2]
    past_len = page_table.shape[1] * page
    sb_w, mb_w, mem_w = SB_HEADS * HEAD_DIM, MB_HEADS * HEAD_DIM, MEM_HEADS * MEM_HEAD_DIM
    assert sb_w == mb_w == mem_w == 1024 and d_model % 1024 == 0
    assert dec_seq == 1 and past_len % MOBA_BLOCK == 0 and seq % MOBA_BLOCK == 0
    tn = 1024

    names = ["qa", "ka", "va", "ga", "qb", "kb", "vb", "gb", "qm", "gm"]
    col = {nm: i * tn for i, nm in enumerate(names)}
    m0 = len(names) * tn
    cols = {"gates": (col["ga"], col["gb"], col["gm"]),
            "merges": (m0, m0 + d_model, m0 + 2 * d_model)}
    in_kinds = ((ID,), (ID,), (ID,), (SILU,),
                (HNORM_ROPE, HEAD_DIM, 0), (HNORM_ROPE, HEAD_DIM, 1), (ID,), (SILU,),
                (HNORM, MEM_HEAD_DIM, 2), (SILU,)) + ((SIGMOID,),) * (3 * d_model // tn)
    mem_kinds = ((HNORM, MEM_HEAD_DIM, 3), (ID,))

    pad = lambda g: jnp.pad(g, (0, MEM_HEAD_DIM - g.shape[0]))
    gains = jnp.stack([pad(qn_moba), pad(kn_moba), qn_mem, kn_mem])
    w_in_b, w_mem_b = w_in.astype(BF16), w_mem_kv.astype(BF16)
    wa, wb, wm, wo = (w.astype(BF16) for w in (w_br_sb, w_br_moba, w_br_mem, w_out))

    memkv = _normed_proj(mem_prompt.reshape(batch * n_mem, d_model), mem_norm_g, w_mem_b, gains,
                         mem_kinds, tm=n_mem, tn=tn, seq=n_mem, pos0=0)

    def prompt_attend(proj):
        oa = _sb_prompt(proj, batch=batch, seq=seq, heads=SB_HEADS, q_col=col["qa"],
                        k_col=col["ka"], v_col=col["va"], tq=256)
        ob = _moba_prompt(proj, batch=batch, seq=seq, heads=MB_HEADS, q_col=col["qb"],
                          k_col=col["kb"], v_col=col["vb"])
        om = _mem_prompt(proj, memkv, batch=batch, seq=seq, n_mem=n_mem, heads=MEM_HEADS,
                         q_col=col["qm"], tq=512)
        return oa, ob, om

    proj_p, y_p = _layer(
        x_prompt.reshape(batch * seq, d_model),
        lambda x: _normed_proj(x, norm_g, w_in_b, gains, in_kinds, tm=512, tn=tn, seq=seq, pos0=0),
        prompt_attend, (wa, wb, wm, wo, cols, 512))

    seg = lambda proj, nm: proj[:, col[nm]:col[nm] + tn]
    n_dec = dec_batch * dec_seq

    def sample_attend(proj):
        hseg = lambda nm, nh: seg(proj, nm).reshape(n_dec, nh, tn // nh)
        pool = lambda c: c.reshape(n_pool, page * c.shape[2], c.shape[3])
        oa = _sb_sample(page_table, hseg("qa", SB_HEADS), hseg("ka", SB_HEADS),
                        hseg("va", SB_HEADS), pool(cache_sb_k), pool(cache_sb_v))
        ob = _moba_sample(page_table, hseg("qb", MB_HEADS), hseg("kb", MB_HEADS),
                          hseg("vb", MB_HEADS), pool(cache_moba_k), pool(cache_moba_v))
        om = _mem_sample(hseg("qm", MEM_HEADS),
                         cache_mem_k.reshape(dec_batch, n_mem * MEM_HEADS, MEM_HEAD_DIM),
                         cache_mem_v.reshape(dec_batch, n_mem * MEM_HEADS, MEM_HEAD_DIM))
        return tuple(o.reshape(n_dec, tn) for o in (oa, ob, om))

    proj_s, y_s = _layer(
        x_sample.reshape(n_dec, d_model),
        lambda x: _normed_proj(x, norm_g, w_in_b, gains, in_kinds, tm=n_dec, tn=tn,
                               seq=dec_seq, pos0=past_len),
        sample_attend, (wa, wb, wm, wo, cols, n_dec))

    heads4 = lambda a, b, t, h, d: a.reshape(b, t, h, d)
    kv_p = [heads4(seg(proj_p, nm), batch, seq, SB_HEADS, HEAD_DIM) for nm in ("ka", "va", "kb", "vb")]
    kv_s = [heads4(seg(proj_s, nm), dec_batch, dec_seq, SB_HEADS, HEAD_DIM)
            for nm in ("ka", "va", "kb", "vb")]
    mk_p = memkv[:, :mem_w].reshape(batch, n_mem, MEM_HEADS, MEM_HEAD_DIM)
    mv_p = memkv[:, mem_w:].reshape(batch, n_mem, MEM_HEADS, MEM_HEAD_DIM)
    return (y_p.reshape(batch, seq, d_model), y_s.reshape(dec_batch, dec_seq, d_model),
            *kv_p, mk_p, mv_p, *kv_s)
```

```python
import functools
import math

import jax
import jax.numpy as jnp
from jax import lax
from jax.experimental import pallas as pl
from jax.experimental.pallas import tpu as pltpu

F32 = jnp.float32
BF16 = jnp.bfloat16

HEAD_DIM = 128
SB_HEADS = 8
MB_HEADS = 8
MEM_HEADS = 4
MEM_HEAD_DIM = 256
MOBA_BLOCK = 256
MOBA_TOPK = 3
ROPE_THETA = 500000.0
ROPE_DIM = HEAD_DIM // 4
EPS = 1e-6

LANES = 128
NEG = -1e30
SB_UNDERFLOW = -105.0
VMEM_LIMIT = 56 * 1024 * 1024

ID, SILU, SIGMOID, HNORM_ROPE, HNORM = range(5)

SUBLANES = 8


def _row_tile(rows, target):
    tile = min(rows, target)
    while rows % tile or tile % SUBLANES:
        tile -= 1
    return tile


def _nt_dot(a, b):
    return lax.dot_general(a, b, (((1,), (1,)), ((), ())), preferred_element_type=F32)


def _dot(a, b):
    return jnp.dot(a, b, preferred_element_type=F32)


def _sigmoid(x):
    return 1.0 / (1.0 + jnp.exp(-x))


def _split_bf16(x, parts):
    out = []
    for _ in range(parts - 1):
        hi = x.astype(BF16)
        out.append(hi)
        x = x - hi.astype(F32)
    out.append(x.astype(BF16))
    return out


def _log_sigmoids(z):
    sp = jnp.log1p(jnp.exp(-jnp.abs(z)))
    return jnp.minimum(z, 0.0) - sp, jnp.minimum(-z, 0.0) - sp


def _rope_tables(pos, half, theta):
    rows = pos.shape[0]
    lane = lax.broadcasted_iota(jnp.int32, (rows, LANES), 1)
    freq = (lane % half).astype(F32) * (1.0 / half)
    ang = pos.astype(F32) * jnp.exp(freq * (-math.log(theta)))
    in_rope = lane < 2 * half
    cos = jnp.where(in_rope, jnp.cos(ang), 1.0)
    sin = jnp.where(in_rope, jnp.sin(ang), 0.0)
    s_lo = jnp.where(lane < half, -sin, 0.0)
    s_hi = jnp.where(lane >= half, sin, 0.0)
    return cos, s_lo, s_hi


def _proj_kernel(x_ref, g_ref, w_ref, gains_ref, o_ref, h_ref, *, kinds, tm, seq, pos0):
    i = pl.program_id(0)
    j = pl.program_id(1)

    @pl.when(j == 0)
    def _():
        x = x_ref[...]
        ms = jnp.mean(x * x, axis=-1, keepdims=True)
        h_ref[...] = ((x * lax.rsqrt(ms + EPS)) * g_ref[...]).astype(BF16)

    def tiles_of(kind_pred):
        cond = None
        for c, kd in enumerate(kinds):
            if kind_pred(kd):
                t = j == c
                cond = t if cond is None else jnp.logical_or(cond, t)
        return cond

    def matmul():
        return _dot(h_ref[...], w_ref[...])

    cond = tiles_of(lambda kd: kd[0] == ID)
    if cond is not None:
        @pl.when(cond)
        def _():
            o_ref[...] = matmul()

    cond = tiles_of(lambda kd: kd[0] == SILU)
    if cond is not None:
        @pl.when(cond)
        def _():
            a = matmul()
            o_ref[...] = a * _sigmoid(a)

    cond = tiles_of(lambda kd: kd[0] == SIGMOID)
    if cond is not None:
        @pl.when(cond)
        def _():
            o_ref[...] = _sigmoid(matmul())

    for c, kd in enumerate(kinds):
        if kd[0] not in (HNORM_ROPE, HNORM):
            continue

        @pl.when(j == c)
        def _(kd=kd):
            a = matmul()
            hd = kd[1]
            gain = gains_ref[kd[2]:kd[2] + 1, :hd]
            if kd[0] == HNORM_ROPE:
                row = lax.broadcasted_iota(jnp.int32, (tm, 1), 0) + i * tm
                pos = pos0 + lax.rem(row, seq)
                cos, s_lo, s_hi = _rope_tables(pos, ROPE_DIM // 2, ROPE_THETA)
            for h in range(a.shape[1] // hd):
                ah = a[:, h * hd:(h + 1) * hd]
                ms = jnp.mean(ah * ah, axis=-1, keepdims=True)
                ah = (ah * lax.rsqrt(ms + EPS)) * gain
                if kd[0] == HNORM_ROPE:
                    half = ROPE_DIM // 2
                    ah = (ah * cos + pltpu.roll(ah, LANES - half, 1) * s_lo
                          + pltpu.roll(ah, half, 1) * s_hi)
                o_ref[:, h * hd:(h + 1) * hd] = ah


def _normed_proj(x, norm_g, w_bf16, gains, kinds, *, tm, tn, seq, pos0):
    r, d = x.shape
    n = w_bf16.shape[1]
    assert r % tm == 0 and n % tn == 0 and len(kinds) == n // tn
    return pl.pallas_call(
        functools.partial(_proj_kernel, kinds=kinds, tm=tm, seq=seq, pos0=pos0),
        grid=(r // tm, n // tn),
        in_specs=[
            pl.BlockSpec((tm, d), lambda i, j: (i, 0)),
            pl.BlockSpec((1, d), lambda i, j: (0, 0)),
            pl.BlockSpec((d, tn), lambda i, j: (0, j)),
            pl.BlockSpec(gains.shape, lambda i, j: (0, 0)),
        ],
        out_specs=pl.BlockSpec((tm, tn), lambda i, j: (i, j)),
        out_shape=jax.ShapeDtypeStruct((r, n), F32),
        scratch_shapes=[pltpu.VMEM((tm, d), BF16)],
        compiler_params=pltpu.CompilerParams(
            dimension_semantics=("arbitrary", "arbitrary"), vmem_limit_bytes=VMEM_LIMIT),
        name="normed_proj",
    )(x, norm_g.reshape(1, d), w_bf16, gains)


def _sb_prompt_kernel(q_ref, k_ref, v_ref, o_ref, *, tq, scale):
    seq = q_ref.shape[0]
    row = lax.broadcasted_iota(jnp.int32, (tq, tq), 0)
    col = lax.broadcasted_iota(jnp.int32, (tq, tq), 1)
    later = (row > col).astype(BF16)
    strict = col < row

    def log_terms(qb, j):
        start = pl.multiple_of(j * tq, tq)
        return _log_sigmoids(_nt_dot(qb, k_ref[pl.ds(start, tq), :].astype(BF16)) * scale)

    def weights(ls_pos, l_neg, c):
        l_hi, l_lo = _split_bf16(l_neg, 2)
        return jnp.exp(ls_pos + _dot(l_hi, later) + _dot(l_lo, later) + c)

    def values(a, j):
        start = pl.multiple_of(j * tq, tq)
        return _dot(a.astype(BF16), v_ref[pl.ds(start, tq), :].astype(BF16))

    def q_tile(qi, carry):
        qstart = pl.multiple_of(qi * tq, tq)
        qb = q_ref[pl.ds(qstart, tq), :].astype(BF16)
        ls_pos, ls_neg = log_terms(qb, qi)
        l_neg = jnp.where(strict, ls_neg, 0.0)
        acc = values(jnp.where(strict, weights(ls_pos, l_neg, 0.0), 0.0), qi)
        c = jnp.sum(l_neg, axis=1, keepdims=True)
        has_prev = (jnp.zeros_like(row) + qi) > 0
        jp = jnp.maximum(qi - 1, 0)
        ls_pos, ls_neg = log_terms(qb, jp)
        l_neg = jnp.where(has_prev, ls_neg, 0.0)
        acc = acc + values(jnp.where(has_prev, weights(ls_pos, l_neg, c), 0.0), jp)
        c = c + jnp.sum(l_neg, axis=1, keepdims=True)

        def more(state):
            j, c_max, _, _ = state
            return jnp.logical_and(j >= 0, c_max > SB_UNDERFLOW)

        def tile(state):
            j, _, c, acc = state
            ls_pos, l_neg = log_terms(qb, j)
            acc = acc + values(weights(ls_pos, l_neg, c), j)
            c = c + jnp.sum(l_neg, axis=1, keepdims=True)
            return j - 1, jnp.max(c), c, acc

        _, _, _, acc = lax.while_loop(more, tile, (qi - 2, jnp.max(c), c, acc))
        o_ref[pl.ds(qstart, tq), :] = acc
        return carry

    lax.fori_loop(0, seq // tq, q_tile, 0)


def _sb_prompt(proj, *, batch, seq, heads, q_col, k_col, v_col, tq):
    d = HEAD_DIM
    return pl.pallas_call(
        functools.partial(_sb_prompt_kernel, tq=tq, scale=d ** -0.5),
        grid=(batch, heads),
        in_specs=[
            pl.BlockSpec((seq, d), lambda b, h: (b, q_col // d + h)),
            pl.BlockSpec((seq, d), lambda b, h: (b, k_col // d + h)),
            pl.BlockSpec((seq, d), lambda b, h: (b, v_col // d + h)),
        ],
        out_specs=pl.BlockSpec((seq, d), lambda b, h: (b, h)),
        out_shape=jax.ShapeDtypeStruct((batch * seq, heads * d), F32),
        compiler_params=pltpu.CompilerParams(
            dimension_semantics=("arbitrary", "arbitrary"), vmem_limit_bytes=VMEM_LIMIT),
        name="sb_prompt",
    )(proj, proj, proj)


def _topk_select(g, valid, topk):
    nb = len(g)
    sel = []
    for m in range(nb):
        rank = jnp.zeros_like(g[m])
        for o in range(nb):
            if o == m:
                continue
            beats = g[o] >= g[m] if o < m else g[o] > g[m]
            rank = rank + jnp.where(beats, valid[o], 0.0)
        sel.append(jnp.where(rank < topk, valid[m], 0.0))
    return sel


def _moba_prompt_kernel(q_ref, k_ref, v_ref, o_ref, *, blk, topk, scale):
    seq = q_ref.shape[0]
    nb = seq // blk
    row = lax.broadcasted_iota(jnp.int32, (blk, blk), 0)
    col = lax.broadcasted_iota(jnp.int32, (blk, blk), 1)
    causal = col <= row
    eye = (row == col).astype(BF16)
    km = jnp.concatenate(
        [jnp.mean(k_ref[n * blk:(n + 1) * blk, :], axis=0, keepdims=True) for n in range(nb)],
        axis=0).astype(BF16)

    for i in range(nb):
        q = q_ref[i * blk:(i + 1) * blk, :].astype(BF16)
        width = (i + 1) * blk
        s = _nt_dot(q, k_ref[0:width, :].astype(BF16)) * scale
        tiles = [s[:, n * blk:(n + 1) * blk] for n in range(i + 1)]
        if i > topk:
            g = _nt_dot(km, q)
            sel_t = jnp.concatenate(
                _topk_select([g[n:n + 1, :] for n in range(i)], [1.0] * i, topk)
                + [jnp.zeros((nb - i, blk), F32)], axis=0)
            sel_c = _nt_dot(eye, sel_t.astype(BF16))
            for n in range(i):
                tiles[n] = jnp.where(sel_c[:, n:n + 1] > 0.5, tiles[n], NEG)
        tiles[i] = jnp.where(causal, tiles[i], NEG)
        m = tiles[0].max(axis=1, keepdims=True)
        for t in tiles[1:]:
            m = jnp.maximum(m, t.max(axis=1, keepdims=True))
        p = [jnp.exp(t - m) for t in tiles]
        l = sum(t.sum(axis=1, keepdims=True) for t in p)
        pv = _dot(jnp.concatenate([t.astype(BF16) for t in p], axis=1),
                  v_ref[0:width, :].astype(BF16))
        o_ref[i * blk:(i + 1) * blk, :] = pv / l


def _moba_prompt(proj, *, batch, seq, heads, q_col, k_col, v_col):
    d = HEAD_DIM
    return pl.pallas_call(
        functools.partial(_moba_prompt_kernel, blk=MOBA_BLOCK, topk=MOBA_TOPK, scale=d ** -0.5),
        grid=(batch, heads),
        in_specs=[
            pl.BlockSpec((seq, d), lambda b, h: (b, q_col // d + h)),
            pl.BlockSpec((seq, d), lambda b, h: (b, k_col // d + h)),
            pl.BlockSpec((seq, d), lambda b, h: (b, v_col // d + h)),
        ],
        out_specs=pl.BlockSpec((seq, d), lambda b, h: (b, h)),
        out_shape=jax.ShapeDtypeStruct((batch * seq, heads * d), F32),
        compiler_params=pltpu.CompilerParams(
            dimension_semantics=("arbitrary", "arbitrary"), vmem_limit_bytes=VMEM_LIMIT),
        name="moba_prompt",
    )(proj, proj, proj)


def _mem_prompt_kernel(q_ref, k_ref, v_ref, o_ref, *, scale):
    s = _nt_dot(q_ref[...].astype(BF16), k_ref[...].astype(BF16)) * scale
    m = jnp.max(s, axis=1, keepdims=True)
    p = jnp.exp(s - m)
    l = jnp.sum(p, axis=1, keepdims=True)
    o_ref[...] = _dot(p.astype(BF16), v_ref[...].astype(BF16)) / l


def _mem_prompt(proj, memkv, *, batch, seq, n_mem, heads, q_col, tq):
    d = MEM_HEAD_DIM
    nq = seq // tq
    return pl.pallas_call(
        functools.partial(_mem_prompt_kernel, scale=d ** -0.5),
        grid=(batch, heads, nq),
        in_specs=[
            pl.BlockSpec((tq, d), lambda b, h, i: (b * nq + i, q_col // d + h)),
            pl.BlockSpec((n_mem, d), lambda b, h, i: (b, h)),
            pl.BlockSpec((n_mem, d), lambda b, h, i: (b, heads + h)),
        ],
        out_specs=pl.BlockSpec((tq, d), lambda b, h, i: (b * nq + i, h)),
        out_shape=jax.ShapeDtypeStruct((batch * seq, heads * d), F32),
        compiler_params=pltpu.CompilerParams(
            dimension_semantics=("arbitrary", "arbitrary", "arbitrary"),
            vmem_limit_bytes=VMEM_LIMIT),
        name="mem_prompt",
    )(proj, memkv, memkv)


def _diag_mask(nh, width):
    sub = lax.broadcasted_iota(jnp.int32, (nh, width), 0)
    lane = lax.broadcasted_iota(jnp.int32, (nh, width), 1)
    return (lane % nh) == sub


def _class_scan(x, nh, cyclic):
    width = x.shape[-1]
    lane = lax.broadcasted_iota(jnp.int32, x.shape, x.ndim - 1)
    shift = nh
    while shift < width:
        nxt = pltpu.roll(x, width - shift, x.ndim - 1)
        x = x + (nxt if cyclic else jnp.where(lane + shift < width, nxt, 0.0))
        shift *= 2
    return x


def _page_specs(n_pages, rows, d):
    def spec(p):
        return pl.BlockSpec((1, rows, d), lambda b, pt: (pt[b, p], 0, 0))
    return [spec(p) for p in range(n_pages)]


def _sb_sample_kernel(pt_ref, q_ref, kn_ref, vn_ref, *refs, n_pages, scale, past_len):
    del pt_ref
    k_refs, v_refs = refs[:n_pages], refs[n_pages:2 * n_pages]
    o_ref, z_ref = refs[2 * n_pages:]
    nh, d = q_ref.shape[1:]
    width = k_refs[0].shape[1]
    slots = width // nh
    q = q_ref[0].astype(BF16)
    diag = _diag_mask(nh, width)

    def diag_row(scores):
        return jnp.sum(jnp.where(diag[:, :scores.shape[1]], scores, 0.0), axis=0, keepdims=True)

    for p in range(n_pages):
        z_ref[p:p + 1, :] = diag_row(_nt_dot(q, k_refs[p][0].astype(BF16)))
    z = z_ref[...] * scale

    lane = lax.broadcasted_iota(jnp.int32, (1, width), 1)
    prow = lax.broadcasted_iota(jnp.int32, (n_pages, width), 0)
    plane = lax.broadcasted_iota(jnp.int32, (n_pages, width), 1)
    q_pos = past_len

    kn = jnp.concatenate([kn_ref[0], jnp.zeros((LANES - nh, d), F32)], axis=0).astype(BF16)
    vn = jnp.concatenate([vn_ref[0], jnp.zeros((LANES - nh, d), F32)], axis=0).astype(BF16)
    z_n = jnp.concatenate([diag_row(_nt_dot(q, kn)), jnp.zeros((1, width - LANES), F32)],
                          axis=1) * scale
    mask_n = (past_len + lane // nh) < q_pos
    lsp_n, lsn_n = _log_sigmoids(z_n)
    l_neg_n = jnp.where(mask_n, lsn_n, 0.0)
    a_n = jnp.where(mask_n, jnp.exp(lsp_n), 0.0)
    total_n = _class_scan(l_neg_n, nh, cyclic=True)

    mask = (prow * slots + plane // nh) < q_pos
    ls_pos, ls_neg = _log_sigmoids(z)
    l_neg = jnp.where(mask, ls_neg, 0.0)
    within = _class_scan(l_neg, nh, cyclic=False) - l_neg
    totals = _class_scan(l_neg, nh, cyclic=True)
    pr = lax.broadcasted_iota(jnp.int32, (n_pages, n_pages), 0)
    pc = lax.broadcasted_iota(jnp.int32, (n_pages, n_pages), 1)
    later = (pc > pr).astype(BF16)
    carry = sum(_dot(later, part) for part in _split_bf16(totals, 3))
    a = jnp.where(mask, jnp.exp(ls_pos + within + carry + total_n), 0.0)

    def spread(w_row):
        w = w_row.shape[1]
        return jnp.where(diag[:, :w], jnp.broadcast_to(w_row, (nh, w)), 0.0).astype(BF16)

    acc = _dot(spread(a_n[:, :LANES]), vn)
    for p in range(n_pages):
        acc = acc + _dot(spread(a[p:p + 1, :]), v_refs[p][0].astype(BF16))
    o_ref[0] = acc


def _sb_sample(page_table, q, k_new, v_new, k_pool, v_pool):
    b, nh, d = q.shape
    n_pages = page_table.shape[1]
    rows = k_pool.shape[1]
    row_spec = pl.BlockSpec((1, nh, d), lambda i, pt: (i, 0, 0))
    pages = _page_specs(n_pages, rows, d)
    grid_spec = pltpu.PrefetchScalarGridSpec(
        num_scalar_prefetch=1,
        grid=(b,),
        in_specs=[row_spec, row_spec, row_spec] + pages + pages,
        out_specs=row_spec,
        scratch_shapes=[pltpu.VMEM((n_pages, rows), F32)],
    )
    return pl.pallas_call(
        functools.partial(_sb_sample_kernel, n_pages=n_pages, scale=d ** -0.5,
                          past_len=n_pages * (rows // nh)),
        grid_spec=grid_spec,
        out_shape=jax.ShapeDtypeStruct((b, nh, d), F32),
        compiler_params=pltpu.CompilerParams(
            dimension_semantics=("arbitrary",), vmem_limit_bytes=VMEM_LIMIT),
        name="sb_sample",
    )(page_table, q, k_new, v_new, *([k_pool] * n_pages), *([v_pool] * n_pages))


def _moba_sample_kernel(pt_ref, q_ref, kn_ref, vn_ref, *refs, n_pages, ppb, topk, scale):
    del pt_ref
    k_refs, v_refs = refs[:n_pages], refs[n_pages:2 * n_pages]
    o_ref, s_ref = refs[2 * n_pages:]
    nh, d = q_ref.shape[1:]
    width = k_refs[0].shape[1]
    slots = width // nh
    nb = n_pages // ppb
    q = q_ref[0].astype(BF16)
    qf = q.astype(F32)
    diag = _diag_mask(nh, width)

    km = []
    for p in range(n_pages):
        kp = k_refs[p][0]
        s_ref[p] = _nt_dot(q, kp.astype(BF16)) * scale
        ksum = jnp.sum(kp.reshape(slots, nh, d), axis=0)
        if p % ppb == 0:
            km.append(ksum)
        else:
            km[-1] = km[-1] + ksum
    g = [jnp.sum(qf * (kb * (1.0 / (ppb * slots))).astype(BF16).astype(F32),
                 axis=1, keepdims=True) for kb in km]
    sel = _topk_select(g, [1.0] * nb, topk)

    s_new = jnp.sum(qf * kn_ref[0].astype(BF16).astype(F32), axis=1, keepdims=True) * scale
    m = s_new
    for p in range(n_pages):
        sp = jnp.where(jnp.logical_and(diag, sel[p // ppb] > 0.5), s_ref[p], NEG)
        s_ref[p] = sp
        m = jnp.maximum(m, jnp.max(sp, axis=1, keepdims=True))
    w_new = jnp.exp(s_new - m)
    l = w_new
    acc = w_new * vn_ref[0]
    for p in range(n_pages):
        pp = jnp.exp(s_ref[p] - m)
        l = l + jnp.sum(pp, axis=1, keepdims=True)
        acc = acc + _dot(pp.astype(BF16), v_refs[p][0].astype(BF16))
    o_ref[0] = acc / l


def _moba_sample(page_table, q, k_new, v_new, k_pool, v_pool):
    b, nh, d = q.shape
    n_pages = page_table.shape[1]
    rows = k_pool.shape[1]
    slots = rows // nh
    ppb = MOBA_BLOCK // slots
    assert MOBA_BLOCK % slots == 0 and n_pages % ppb == 0
    row_spec = pl.BlockSpec((1, nh, d), lambda i, pt: (i, 0, 0))
    pages = _page_specs(n_pages, rows, d)
    grid_spec = pltpu.PrefetchScalarGridSpec(
        num_scalar_prefetch=1,
        grid=(b,),
        in_specs=[row_spec, row_spec, row_spec] + pages + pages,
        out_specs=row_spec,
        scratch_shapes=[pltpu.VMEM((n_pages, nh, rows), F32)],
    )
    return pl.pallas_call(
        functools.partial(_moba_sample_kernel, n_pages=n_pages, ppb=ppb, topk=MOBA_TOPK,
                          scale=d ** -0.5),
        grid_spec=grid_spec,
        out_shape=jax.ShapeDtypeStruct((b, nh, d), F32),
        compiler_params=pltpu.CompilerParams(
            dimension_semantics=("arbitrary",), vmem_limit_bytes=VMEM_LIMIT),
        name="moba_sample",
    )(page_table, q, k_new, v_new, *([k_pool] * n_pages), *([v_pool] * n_pages))


def _mem_sample_kernel(q_ref, k_ref, v_ref, o_ref, *, scale):
    s = jnp.sum(k_ref[0] * q_ref[...], axis=-1, keepdims=True) * scale
    m = jnp.max(s, axis=0, keepdims=True)
    p = jnp.exp(s - m)
    l = jnp.sum(p, axis=0)
    o_ref[0] = jnp.sum(p * v_ref[0], axis=0) / l


def _mem_sample(q, k_cache, v_cache):
    b, nh, d = q.shape
    n_mem = k_cache.shape[1]
    row_spec = pl.BlockSpec((1, nh, d), lambda i: (i, 0, 0))
    cache_spec = pl.BlockSpec((1, n_mem, nh, d), lambda i: (i, 0, 0, 0))
    return pl.pallas_call(
        functools.partial(_mem_sample_kernel, scale=d ** -0.5),
        grid=(b,),
        in_specs=[row_spec, cache_spec, cache_spec],
        out_specs=row_spec,
        out_shape=jax.ShapeDtypeStruct((b, nh, d), F32),
        compiler_params=pltpu.CompilerParams(
            dimension_semantics=("arbitrary",), vmem_limit_bytes=VMEM_LIMIT),
        name="mem_sample",
    )(q, k_cache, v_cache)


def _merge_kernel(oa_ref, ob_ref, om_ref, ga_ref, gb_ref, gm_ref, sa_ref, sb_ref, sm_ref,
                  wa_ref, wb_ref, wm_ref, o_ref):
    def branch(o_r, g_r, s_r, w_r):
        return s_r[...] * _dot((o_r[...] * g_r[...]).astype(BF16), w_r[...])
    merged = branch(oa_ref, ga_ref, sa_ref, wa_ref)
    merged = merged + branch(ob_ref, gb_ref, sb_ref, wb_ref)
    merged = merged + branch(om_ref, gm_ref, sm_ref, wm_ref)
    o_ref[...] = merged.astype(BF16)


def _merge(oa, ob, om, proj, wa, wb, wm, *, gate_cols, merge_cols, tm, tn):
    r, e = oa.shape
    d = wa.shape[1]
    assert r % tm == 0 and d % tn == 0
    act = pl.BlockSpec((tm, e), lambda i, n: (i, 0))
    gates = [pl.BlockSpec((tm, e), functools.partial(lambda i, n, c: (i, c), c=c // e))
             for c in gate_cols]
    merges = [pl.BlockSpec((tm, tn), functools.partial(lambda i, n, c: (i, c + n), c=c // tn))
              for c in merge_cols]
    wspec = pl.BlockSpec((e, tn), lambda i, n: (0, n))
    return pl.pallas_call(
        _merge_kernel,
        grid=(r // tm, d // tn),
        in_specs=[act, act, act] + gates + merges + [wspec, wspec, wspec],
        out_specs=pl.BlockSpec((tm, tn), lambda i, n: (i, n)),
        out_shape=jax.ShapeDtypeStruct((r, d), BF16),
        compiler_params=pltpu.CompilerParams(
            dimension_semantics=("arbitrary", "arbitrary"), vmem_limit_bytes=VMEM_LIMIT),
        name="merge",
    )(oa, ob, om, proj, proj, proj, proj, proj, proj, wa, wb, wm)


def _out_kernel(m_ref, w_ref, x_ref, o_ref):
    o_ref[...] = x_ref[...] + _dot(m_ref[...], w_ref[...])


def _out_proj(merged, w_out, x, *, tm, tn):
    r, d = merged.shape
    n = w_out.shape[1]
    assert r % tm == 0 and n % tn == 0
    return pl.pallas_call(
        _out_kernel,
        grid=(r // tm, n // tn),
        in_specs=[
            pl.BlockSpec((tm, d), lambda i, j: (i, 0)),
            pl.BlockSpec((d, tn), lambda i, j: (0, j)),
            pl.BlockSpec((tm, tn), lambda i, j: (i, j)),
        ],
        out_specs=pl.BlockSpec((tm, tn), lambda i, j: (i, j)),
        out_shape=jax.ShapeDtypeStruct((r, n), F32),
        compiler_params=pltpu.CompilerParams(
            dimension_semantics=("arbitrary", "arbitrary"), vmem_limit_bytes=VMEM_LIMIT),
        name="out_proj",
    )(merged, w_out, x)


def _layer(x, proj_fn, attend_fn, weights):
    wa, wb, wm, w_out, cols, tm = weights
    proj = proj_fn(x)
    oa, ob, om = attend_fn(proj)
    merged = _merge(oa, ob, om, proj, wa, wb, wm, gate_cols=cols["gates"],
                    merge_cols=cols["merges"], tm=tm, tn=1024)
    return proj, _out_proj(merged, w_out, x, tm=tm, tn=1024)


def kernel(x_prompt, x_sample, mem_prompt, cache_sb_k, cache_sb_v, cache_moba_k, cache_moba_v,
           cache_mem_k, cache_mem_v, page_table, norm_g, w_in, qn_moba, kn_moba, qn_mem, kn_mem,
           mem_norm_g, w_mem_kv, w_br_sb, w_br_moba, w_br_mem, w_out):
    batch, seq, d_model = x_prompt.shape
    dec_batch, dec_seq, _ = x_sample.shape
    n_mem = mem_prompt.shape[1]
    n_pool, page = cache_sb_k.shape[:2]
    past_len = page_table.shape[1] * page
    sb_w, mb_w, mem_w = SB_HEADS * HEAD_DIM, MB_HEADS * HEAD_DIM, MEM_HEADS * MEM_HEAD_DIM
    assert sb_w == mb_w == mem_w == 1024 and d_model % 1024 == 0
    assert dec_seq == 1 and past_len % MOBA_BLOCK == 0 and seq % MOBA_BLOCK == 0
    tn = 1024

    names = ["qa", "ka", "va", "ga", "qb", "kb", "vb", "gb", "qm", "gm"]
    col = {nm: i * tn for i, nm in enumerate(names)}
    m0 = len(names) * tn
    cols = {"gates": (col["ga"], col["gb"], col["gm"]),
            "merges": (m0, m0 + d_model, m0 + 2 * d_model)}
    in_kinds = ((ID,), (ID,), (ID,), (SILU,),
                (HNORM_ROPE, HEAD_DIM, 0), (HNORM_ROPE, HEAD_DIM, 1), (ID,), (SILU,),
                (HNORM, MEM_HEAD_DIM, 2), (SILU,)) + ((SIGMOID,),) * (3 * d_model // tn)
    mem_kinds = ((HNORM, MEM_HEAD_DIM, 3), (ID,))

    pad = lambda g: jnp.pad(g, (0, MEM_HEAD_DIM - g.shape[0]))
    gains = jnp.stack([pad(qn_moba), pad(kn_moba), qn_mem, kn_mem])
    w_in_b, w_mem_b = w_in.astype(BF16), w_mem_kv.astype(BF16)
    wa, wb, wm, wo = (w.astype(BF16) for w in (w_br_sb, w_br_moba, w_br_mem, w_out))

    memkv = _normed_proj(mem_prompt.reshape(batch * n_mem, d_model), mem_norm_g, w_mem_b, gains,
                         mem_kinds, tm=n_mem, tn=tn, seq=n_mem, pos0=0)

    def prompt_attend(proj):
        oa = _sb_prompt(proj, batch=batch, seq=seq, heads=SB_HEADS, q_col=col["qa"],
                        k_col=col["ka"], v_col=col["va"], tq=256)
        ob = _moba_prompt(proj, batch=batch, seq=seq, heads=MB_HEADS, q_col=col["qb"],
                          k_col=col["kb"], v_col=col["vb"])
        om = _mem_prompt(proj, memkv, batch=batch, seq=seq, n_mem=n_mem, heads=MEM_HEADS,
                         q_col=col["qm"], tq=512)
        return oa, ob, om

    proj_p, y_p = _layer(
        x_prompt.reshape(batch * seq, d_model),
        lambda x: _normed_proj(x, norm_g, w_in_b, gains, in_kinds, tm=_row_tile(batch * seq, 1024),
                               tn=tn, seq=seq, pos0=0),
        prompt_attend, (wa, wb, wm, wo, cols, _row_tile(batch * seq, 512)))

    seg = lambda proj, nm: proj[:, col[nm]:col[nm] + tn]
    n_dec = dec_batch * dec_seq

    def sample_attend(proj):
        hseg = lambda nm, nh: seg(proj, nm).reshape(n_dec, nh, tn // nh)
        pool = lambda c: c.reshape(n_pool, page * c.shape[2], c.shape[3])
        oa = _sb_sample(page_table, hseg("qa", SB_HEADS), hseg("ka", SB_HEADS),
                        hseg("va", SB_HEADS), pool(cache_sb_k), pool(cache_sb_v))
        ob = _moba_sample(page_table, hseg("qb", MB_HEADS), hseg("kb", MB_HEADS),
                          hseg("vb", MB_HEADS), pool(cache_moba_k), pool(cache_moba_v))
        om = _mem_sample(hseg("qm", MEM_HEADS), cache_mem_k, cache_mem_v)
        return tuple(o.reshape(n_dec, tn) for o in (oa, ob, om))

    proj_s, y_s = _layer(
        x_sample.reshape(n_dec, d_model),
        lambda x: _normed_proj(x, norm_g, w_in_b, gains, in_kinds, tm=n_dec, tn=tn,
                               seq=dec_seq, pos0=past_len),
        sample_attend, (wa, wb, wm, wo, cols, n_dec))

    heads4 = lambda a, b, t, h, d: a.reshape(b, t, h, d)
    kv_p = [heads4(seg(proj_p, nm), batch, seq, SB_HEADS, HEAD_DIM) for nm in ("ka", "va", "kb", "vb")]
    kv_s = [heads4(seg(proj_s, nm), dec_batch, dec_seq, SB_HEADS, HEAD_DIM)
            for nm in ("ka", "va", "kb", "vb")]
    mk_p = memkv[:, :mem_w].reshape(batch, n_mem, MEM_HEADS, MEM_HEAD_DIM)
    mv_p = memkv[:, mem_w:].reshape(batch, n_mem, MEM_HEADS, MEM_HEAD_DIM)
    return (y_p.reshape(batch, seq, d_model), y_s.reshape(dec_batch, dec_seq, d_model),
            *kv_p, mk_p, mv_p, *kv_s)
```

```python
import functools
import math

import jax
import jax.numpy as jnp
from jax import lax
from jax.experimental import pallas as pl
from jax.experimental.pallas import tpu as pltpu

F32 = jnp.float32
BF16 = jnp.bfloat16

HEAD_DIM = 128
SB_HEADS = 8
MB_HEADS = 8
MEM_HEADS = 4
MEM_HEAD_DIM = 256
MOBA_BLOCK = 256
MOBA_TOPK = 3
ROPE_THETA = 500000.0
ROPE_DIM = HEAD_DIM // 4
EPS = 1e-6

LANES = 128
SUBLANES = 8
NEG = -1e30
SB_UNDERFLOW = -105.0
SB_SAMPLE_HEAD_PAGES = 2
VMEM_LIMIT = 56 * 1024 * 1024

ID, SILU, SIGMOID, HNORM_ROPE, HNORM = range(5)


def _row_tile(rows, target):
    tile = min(rows, target)
    while rows % tile or tile % SUBLANES:
        tile -= 1
    return tile


def _nt_dot(a, b):
    return lax.dot_general(a, b, (((1,), (1,)), ((), ())), preferred_element_type=F32)


def _dot(a, b):
    return jnp.dot(a, b, preferred_element_type=F32)


def _sigmoid(x):
    return 1.0 / (1.0 + jnp.exp(-x))


def _split_bf16(x, parts):
    out = []
    for _ in range(parts - 1):
        hi = x.astype(BF16)
        out.append(hi)
        x = x - hi.astype(F32)
    out.append(x.astype(BF16))
    return out


def _log_sigmoids(z):
    sp = jnp.log1p(jnp.exp(-jnp.abs(z)))
    return jnp.minimum(z, 0.0) - sp, jnp.minimum(-z, 0.0) - sp


def _seg_spec(block, index):
    return pl.BlockSpec((None,) + block, index)


def _rope_angles(pos, half, theta):
    lane = lax.broadcasted_iota(jnp.int32, (1, LANES), 1)
    inv_freq = jnp.exp((lane % half).astype(F32) * (-math.log(theta) / half))
    ang = pos * inv_freq
    return jnp.cos(ang), jnp.sin(ang)


def _proj_kernel(x_ref, g_ref, w_ref, gains_ref, o_ref, h_ref, rope_ref, *, kinds, tm, seq, pos0):
    i = pl.program_id(0)
    j = pl.program_id(1)
    half = ROPE_DIM // 2

    @pl.when(j == 0)
    def _():
        x = x_ref[...]
        ms = jnp.mean(x * x, axis=-1, keepdims=True)
        h_ref[...] = ((x * lax.rsqrt(ms + EPS)) * g_ref[...]).astype(BF16)

    def tiles_of(kind_pred):
        cond = None
        for c, kd in enumerate(kinds):
            if kind_pred(kd):
                t = j == c
                cond = t if cond is None else jnp.logical_or(cond, t)
        return cond

    def matmul():
        return _dot(h_ref[...], w_ref[...])

    cond = tiles_of(lambda kd: kd[0] == ID)
    if cond is not None:
        @pl.when(cond)
        def _():
            o_ref[...] = matmul()

    cond = tiles_of(lambda kd: kd[0] == SILU)
    if cond is not None:
        @pl.when(cond)
        def _():
            a = matmul()
            o_ref[...] = a * _sigmoid(a)

    cond = tiles_of(lambda kd: kd[0] == SIGMOID)
    if cond is not None:
        @pl.when(cond)
        def _():
            o_ref[...] = _sigmoid(matmul())

    rope_tiles = [c for c, kd in enumerate(kinds) if kd[0] == HNORM_ROPE]
    if rope_tiles:
        assert seq % tm == 0 or tm % seq == 0
        local_period = tm if seq % tm == 0 else seq

        @pl.when(jnp.logical_and(i == 0, j == rope_tiles[0]))
        def _():
            r = lax.broadcasted_iota(jnp.int32, (tm, 1), 0)
            cos, sin = _rope_angles(lax.rem(r, local_period).astype(F32), half, ROPE_THETA)
            rope_ref[0] = cos
            rope_ref[1] = sin

    for c, kd in enumerate(kinds):
        if kd[0] not in (HNORM_ROPE, HNORM):
            continue

        @pl.when(j == c)
        def _(kd=kd):
            a = matmul()
            hd = kd[1]
            gain = gains_ref[kd[2]:kd[2] + 1, :hd]
            if kd[0] == HNORM_ROPE:
                base = pos0 + (lax.rem(i * tm, seq) if seq % tm == 0 else 0 * i)
                cos_b, sin_b = _rope_angles(base.astype(F32), half, ROPE_THETA)
                cos_l, sin_l = rope_ref[0], rope_ref[1]
                lane = lax.broadcasted_iota(jnp.int32, (1, LANES), 1)
                in_rope = lane < 2 * half
                cos = jnp.where(in_rope, cos_l * cos_b - sin_l * sin_b, 1.0)
                sin = jnp.where(in_rope, sin_l * cos_b + cos_l * sin_b, 0.0)
                s_lo = jnp.where(lane < half, -sin, 0.0)
                s_hi = jnp.where(lane >= half, sin, 0.0)
            for h in range(a.shape[1] // hd):
                ah = a[:, h * hd:(h + 1) * hd]
                ms = jnp.mean(ah * ah, axis=-1, keepdims=True)
                ah = (ah * lax.rsqrt(ms + EPS)) * gain
                if kd[0] == HNORM_ROPE:
                    ah = (ah * cos + pltpu.roll(ah, LANES - half, 1) * s_lo
                          + pltpu.roll(ah, half, 1) * s_hi)
                o_ref[:, h * hd:(h + 1) * hd] = ah


def _normed_proj(x, norm_g, w_bf16, gains, kinds, *, tm, tn, seq, pos0):
    r, d = x.shape
    n = w_bf16.shape[1]
    assert r % tm == 0 and n % tn == 0 and len(kinds) == n // tn
    return pl.pallas_call(
        functools.partial(_proj_kernel, kinds=kinds, tm=tm, seq=seq, pos0=pos0),
        grid=(r // tm, n // tn),
        in_specs=[
            pl.BlockSpec((tm, d), lambda i, j: (i, 0)),
            pl.BlockSpec((1, d), lambda i, j: (0, 0)),
            pl.BlockSpec((d, tn), lambda i, j: (0, j)),
            pl.BlockSpec(gains.shape, lambda i, j: (0, 0)),
        ],
        out_specs=_seg_spec((tm, tn), lambda i, j: (j, i, 0)),
        out_shape=jax.ShapeDtypeStruct((n // tn, r, tn), F32),
        scratch_shapes=[pltpu.VMEM((tm, d), BF16), pltpu.VMEM((2, tm, LANES), F32)],
        compiler_params=pltpu.CompilerParams(
            dimension_semantics=("arbitrary", "arbitrary"), vmem_limit_bytes=VMEM_LIMIT),
        name="normed_proj",
    )(x, norm_g.reshape(1, d), w_bf16, gains)


def _sb_prompt_kernel(q_ref, k_ref, v_ref, g_ref, o_ref, *, tq, scale):
    seq = q_ref.shape[0]
    row = lax.broadcasted_iota(jnp.int32, (tq, tq), 0)
    col = lax.broadcasted_iota(jnp.int32, (tq, tq), 1)
    later = (row > col).astype(BF16)
    strict = col < row

    def log_terms(qb, j):
        start = pl.multiple_of(j * tq, tq)
        return _log_sigmoids(_nt_dot(qb, k_ref[pl.ds(start, tq), :].astype(BF16)) * scale)

    def weights(ls_pos, l_neg, c):
        l_hi, l_lo = _split_bf16(l_neg, 2)
        return jnp.exp(ls_pos + _dot(l_hi, later) + _dot(l_lo, later) + c)

    def values(a, j):
        start = pl.multiple_of(j * tq, tq)
        return _dot(a.astype(BF16), v_ref[pl.ds(start, tq), :].astype(BF16))

    def q_tile(qi, carry):
        qstart = pl.multiple_of(qi * tq, tq)
        qb = q_ref[pl.ds(qstart, tq), :].astype(BF16)
        ls_pos, ls_neg = log_terms(qb, qi)
        l_neg = jnp.where(strict, ls_neg, 0.0)
        acc = values(jnp.where(strict, weights(ls_pos, l_neg, 0.0), 0.0), qi)
        c = jnp.sum(l_neg, axis=1, keepdims=True)
        has_prev = (jnp.zeros_like(row) + qi) > 0
        jp = jnp.maximum(qi - 1, 0)
        ls_pos, ls_neg = log_terms(qb, jp)
        l_neg = jnp.where(has_prev, ls_neg, 0.0)
        acc = acc + values(jnp.where(has_prev, weights(ls_pos, l_neg, c), 0.0), jp)
        c = c + jnp.sum(l_neg, axis=1, keepdims=True)

        def more(state):
            j, c_max, _, _ = state
            return jnp.logical_and(j >= 0, c_max > SB_UNDERFLOW)

        def tile(state):
            j, _, c, acc = state
            ls_pos, l_neg = log_terms(qb, j)
            acc = acc + values(weights(ls_pos, l_neg, c), j)
            c = c + jnp.sum(l_neg, axis=1, keepdims=True)
            return j - 1, jnp.max(c), c, acc

        _, _, _, acc = lax.while_loop(more, tile, (qi - 2, jnp.max(c), c, acc))
        o_ref[pl.ds(qstart, tq), :] = (acc * g_ref[pl.ds(qstart, tq), :]).astype(o_ref.dtype)
        return carry

    lax.fori_loop(0, seq // tq, q_tile, 0)


def _head_specs(seq, d, segs):
    return [_seg_spec((seq, d), functools.partial(lambda b, h, s: (s, b, h), s=s)) for s in segs]


def _sb_prompt(proj, *, batch, seq, heads, segs, tq):
    d = HEAD_DIM
    return pl.pallas_call(
        functools.partial(_sb_prompt_kernel, tq=tq, scale=d ** -0.5),
        grid=(batch, heads),
        in_specs=_head_specs(seq, d, segs),
        out_specs=pl.BlockSpec((seq, d), lambda b, h: (b, h)),
        out_shape=jax.ShapeDtypeStruct((batch * seq, heads * d), BF16),
        compiler_params=pltpu.CompilerParams(
            dimension_semantics=("arbitrary", "arbitrary"), vmem_limit_bytes=VMEM_LIMIT),
        name="sb_prompt",
    )(proj, proj, proj, proj)


def _topk_select(g, valid, topk):
    nb = len(g)
    sel = []
    for m in range(nb):
        rank = jnp.zeros_like(g[m])
        for o in range(nb):
            if o == m:
                continue
            beats = g[o] >= g[m] if o < m else g[o] > g[m]
            rank = rank + jnp.where(beats, valid[o], 0.0)
        sel.append(jnp.where(rank < topk, valid[m], 0.0))
    return sel


def _moba_prompt_kernel(q_ref, k_ref, v_ref, g_ref, o_ref, *, blk, topk, scale):
    seq = q_ref.shape[0]
    nb = seq // blk
    row = lax.broadcasted_iota(jnp.int32, (blk, blk), 0)
    col = lax.broadcasted_iota(jnp.int32, (blk, blk), 1)
    causal = col <= row
    eye = (row == col).astype(BF16)
    km = jnp.concatenate(
        [jnp.mean(k_ref[n * blk:(n + 1) * blk, :], axis=0, keepdims=True) for n in range(nb)],
        axis=0).astype(BF16)

    for i in range(nb):
        q = q_ref[i * blk:(i + 1) * blk, :].astype(BF16)
        width = (i + 1) * blk
        s = _nt_dot(q, k_ref[0:width, :].astype(BF16)) * scale
        tiles = [s[:, n * blk:(n + 1) * blk] for n in range(i + 1)]
        if i > topk:
            g = _nt_dot(km, q)
            sel_t = jnp.concatenate(
                _topk_select([g[n:n + 1, :] for n in range(i)], [1.0] * i, topk)
                + [jnp.zeros((nb - i, blk), F32)], axis=0)
            sel_c = _nt_dot(eye, sel_t.astype(BF16))
            for n in range(i):
                tiles[n] = jnp.where(sel_c[:, n:n + 1] > 0.5, tiles[n], NEG)
        tiles[i] = jnp.where(causal, tiles[i], NEG)
        m = tiles[0].max(axis=1, keepdims=True)
        for t in tiles[1:]:
            m = jnp.maximum(m, t.max(axis=1, keepdims=True))
        p = [jnp.exp(t - m) for t in tiles]
        l = sum(t.sum(axis=1, keepdims=True) for t in p)
        pv = _dot(jnp.concatenate([t.astype(BF16) for t in p], axis=1),
                  v_ref[0:width, :].astype(BF16))
        o_ref[i * blk:(i + 1) * blk, :] = (
            (pv / l) * g_ref[i * blk:(i + 1) * blk, :]).astype(o_ref.dtype)


def _moba_prompt(proj, *, batch, seq, heads, segs):
    d = HEAD_DIM
    return pl.pallas_call(
        functools.partial(_moba_prompt_kernel, blk=MOBA_BLOCK, topk=MOBA_TOPK, scale=d ** -0.5),
        grid=(batch, heads),
        in_specs=_head_specs(seq, d, segs),
        out_specs=pl.BlockSpec((seq, d), lambda b, h: (b, h)),
        out_shape=jax.ShapeDtypeStruct((batch * seq, heads * d), BF16),
        compiler_params=pltpu.CompilerParams(
            dimension_semantics=("arbitrary", "arbitrary"), vmem_limit_bytes=VMEM_LIMIT),
        name="moba_prompt",
    )(proj, proj, proj, proj)


def _mem_prompt_kernel(q_ref, k_ref, v_ref, g_ref, o_ref, *, scale):
    s = _nt_dot(q_ref[...].astype(BF16), k_ref[...].astype(BF16)) * scale
    m = jnp.max(s, axis=1, keepdims=True)
    p = jnp.exp(s - m)
    l = jnp.sum(p, axis=1, keepdims=True)
    o = _dot(p.astype(BF16), v_ref[...].astype(BF16)) / l
    o_ref[...] = (o * g_ref[...]).astype(o_ref.dtype)


def _mem_prompt(proj, memkv, *, batch, seq, n_mem, heads, q_seg, g_seg, tq):
    d = MEM_HEAD_DIM
    nq = seq // tq
    return pl.pallas_call(
        functools.partial(_mem_prompt_kernel, scale=d ** -0.5),
        grid=(batch, heads, nq),
        in_specs=[
            _seg_spec((tq, d), lambda b, h, i: (q_seg, b * nq + i, h)),
            _seg_spec((n_mem, d), lambda b, h, i: (0, b, h)),
            _seg_spec((n_mem, d), lambda b, h, i: (1, b, h)),
            _seg_spec((tq, d), lambda b, h, i: (g_seg, b * nq + i, h)),
        ],
        out_specs=pl.BlockSpec((tq, d), lambda b, h, i: (b * nq + i, h)),
        out_shape=jax.ShapeDtypeStruct((batch * seq, heads * d), BF16),
        compiler_params=pltpu.CompilerParams(
            dimension_semantics=("arbitrary", "arbitrary", "arbitrary"),
            vmem_limit_bytes=VMEM_LIMIT),
        name="mem_prompt",
    )(proj, memkv, memkv, proj)


def _diag_mask(nh, width):
    sub = lax.broadcasted_iota(jnp.int32, (nh, width), 0)
    lane = lax.broadcasted_iota(jnp.int32, (nh, width), 1)
    return (lane % nh) == sub


def _class_scan(x, nh, cyclic):
    width = x.shape[-1]
    lane = lax.broadcasted_iota(jnp.int32, x.shape, x.ndim - 1)
    shift = nh
    while shift < width:
        nxt = pltpu.roll(x, width - shift, x.ndim - 1)
        x = x + (nxt if cyclic else jnp.where(lane + shift < width, nxt, 0.0))
        shift *= 2
    return x


def _page_specs(n_pages, rows, d):
    def spec(p):
        return pl.BlockSpec((1, rows, d), lambda b, pt: (pt[b, p], 0, 0))
    return [spec(p) for p in range(n_pages)]


def _sb_pages(q, k_refs, v_refs, z_ref, first_page, c_after, acc, *, scale, q_pos):
    nh = q.shape[0]
    n = len(k_refs)
    width = k_refs[0].shape[1]
    slots = width // nh
    diag = _diag_mask(nh, width)
    for p in range(n):
        z_ref[p:p + 1, :] = jnp.sum(
            jnp.where(diag, _nt_dot(q, k_refs[p][0].astype(BF16)), 0.0), axis=0, keepdims=True)
    z = z_ref[0:n, :] * scale
    prow = lax.broadcasted_iota(jnp.int32, (n, width), 0)
    plane = lax.broadcasted_iota(jnp.int32, (n, width), 1)
    mask = ((first_page + prow) * slots + plane // nh) < q_pos
    ls_pos, ls_neg = _log_sigmoids(z)
    l_neg = jnp.where(mask, ls_neg, 0.0)
    within = _class_scan(l_neg, nh, cyclic=False) - l_neg
    totals = _class_scan(l_neg, nh, cyclic=True)
    pr = lax.broadcasted_iota(jnp.int32, (n, n), 0)
    pc = lax.broadcasted_iota(jnp.int32, (n, n), 1)
    later = (pc > pr).astype(BF16)
    carry = sum(_dot(later, part) for part in _split_bf16(totals, 3))
    a = jnp.where(mask, jnp.exp(ls_pos + within + carry + c_after), 0.0)
    for p in range(n):
        w_p = jnp.where(diag, jnp.broadcast_to(a[p:p + 1, :], (nh, width)), 0.0).astype(BF16)
        acc = acc + _dot(w_p, v_refs[p][0].astype(BF16))
    return acc, c_after + jnp.sum(totals, axis=0, keepdims=True)


def _sb_sample_head_kernel(pt_ref, q_ref, kn_ref, vn_ref, *refs, n_head, first_page, scale, past_len):
    del pt_ref
    k_refs, v_refs = refs[:n_head], refs[n_head:2 * n_head]
    acc_ref, c_ref, more_ref, z_ref = refs[2 * n_head:]
    nh, d = q_ref.shape[1:]
    width = k_refs[0].shape[1]
    q = q_ref[0].astype(BF16)
    q_pos = past_len
    diag = _diag_mask(nh, LANES)
    lane = lax.broadcasted_iota(jnp.int32, (1, width), 1)

    kn = jnp.concatenate([kn_ref[0], jnp.zeros((LANES - nh, d), F32)], axis=0).astype(BF16)
    vn = jnp.concatenate([vn_ref[0], jnp.zeros((LANES - nh, d), F32)], axis=0).astype(BF16)
    z_n = jnp.sum(jnp.where(diag, _nt_dot(q, kn), 0.0), axis=0, keepdims=True)
    z_n = jnp.concatenate([z_n, jnp.zeros((1, width - LANES), F32)], axis=1) * scale
    mask_n = (past_len + lane // nh) < q_pos
    lsp_n, lsn_n = _log_sigmoids(z_n)
    a_n = jnp.where(mask_n, jnp.exp(lsp_n), 0.0)
    c_new = _class_scan(jnp.where(mask_n, lsn_n, 0.0), nh, cyclic=True)
    w_n = jnp.where(diag, jnp.broadcast_to(a_n[:, :LANES], (nh, LANES)), 0.0).astype(BF16)

    acc, c = _sb_pages(q, k_refs, v_refs, z_ref, first_page, c_new, _dot(w_n, vn),
                       scale=scale, q_pos=q_pos)
    acc_ref[0] = acc
    c_ref[0] = c
    more_ref[0] = jnp.where(jnp.max(c) > SB_UNDERFLOW, 1, 0) + jnp.zeros((1, LANES), jnp.int32)


def _sb_sample_tail_kernel(pt_ref, more_ref, src_ref, q_ref, g_ref, acc_ref, c_ref, *refs,
                           n_tail, scale, past_len):
    del pt_ref, src_ref
    k_refs, v_refs = refs[:n_tail], refs[n_tail:2 * n_tail]
    o_ref, z_ref = refs[2 * n_tail:]
    b = pl.program_id(0)

    @pl.when(more_ref[b] == 0)
    def _():
        o_ref[0] = acc_ref[0] * g_ref[0]

    @pl.when(more_ref[b] != 0)
    def _():
        acc, _ = _sb_pages(q_ref[0].astype(BF16), k_refs, v_refs, z_ref, 0, c_ref[0], acc_ref[0],
                           scale=scale, q_pos=past_len)
        o_ref[0] = acc * g_ref[0]


def _sb_sample(page_table, rows4, k_pool, v_pool, *, n_head):
    q, k_new, v_new, gate = rows4
    b, nh, d = q.shape
    n_pages = page_table.shape[1]
    rows = k_pool.shape[1]
    n_tail = n_pages - n_head
    past_len = n_pages * (rows // nh)
    scale = d ** -0.5
    params = pltpu.CompilerParams(dimension_semantics=("arbitrary",), vmem_limit_bytes=VMEM_LIMIT)

    row_spec = pl.BlockSpec((1, nh, d), lambda i, pt: (i, 0, 0))
    head_pages = [pl.BlockSpec((1, rows, d), functools.partial(
        lambda i, pt, p: (pt[i, p], 0, 0), p=n_tail + p)) for p in range(n_head)]
    acc, c, more = pl.pallas_call(
        functools.partial(_sb_sample_head_kernel, n_head=n_head, first_page=n_tail, scale=scale,
                          past_len=past_len),
        grid_spec=pltpu.PrefetchScalarGridSpec(
            num_scalar_prefetch=1,
            grid=(b,),
            in_specs=[row_spec] * 3 + head_pages + head_pages,
            out_specs=[row_spec, pl.BlockSpec((1, 1, rows), lambda i, pt: (i, 0, 0)),
                       pl.BlockSpec((1, 1, LANES), lambda i, pt: (i, 0, 0))],
            scratch_shapes=[pltpu.VMEM((n_head, rows), F32)],
        ),
        out_shape=[jax.ShapeDtypeStruct((b, nh, d), F32), jax.ShapeDtypeStruct((b, 1, rows), F32),
                   jax.ShapeDtypeStruct((b, 1, LANES), jnp.int32)],
        compiler_params=params,
        name="sb_sample_head",
    )(page_table, q, k_new, v_new, *([k_pool] * n_head), *([v_pool] * n_head))

    more = more[:, 0, 0]
    src = lax.cummax(jnp.where(more > 0, jnp.arange(b, dtype=jnp.int32), 0), axis=0)
    row3 = pl.BlockSpec((1, nh, d), lambda i, pt, mo, sr: (i, 0, 0))

    def tail_page(i, pt, mo, sr, p):
        seq_row = jnp.clip(sr[jnp.minimum(i, b - 1)], 0, b - 1)
        return (pt[seq_row, p], 0, 0)

    tail_pages = [pl.BlockSpec((1, rows, d), functools.partial(tail_page, p=p))
                  for p in range(n_tail)]
    return pl.pallas_call(
        functools.partial(_sb_sample_tail_kernel, n_tail=n_tail, scale=scale, past_len=past_len),
        grid_spec=pltpu.PrefetchScalarGridSpec(
            num_scalar_prefetch=3,
            grid=(b,),
            in_specs=[row3, row3, row3, pl.BlockSpec((1, 1, rows), lambda i, pt, mo, sr: (i, 0, 0))]
            + tail_pages + tail_pages,
            out_specs=row3,
            scratch_shapes=[pltpu.VMEM((n_tail, rows), F32)],
        ),
        out_shape=jax.ShapeDtypeStruct((b, nh, d), F32),
        compiler_params=params,
        name="sb_sample_tail",
    )(page_table, more, src, q, gate, acc, c, *([k_pool] * n_tail), *([v_pool] * n_tail))


def _paged_sample_call(kernel_fn, name, page_table, rows4, k_pool, v_pool, scratch):
    b, nh, d = rows4[0].shape
    n_pages = page_table.shape[1]
    rows = k_pool.shape[1]
    row_spec = pl.BlockSpec((1, nh, d), lambda i, pt: (i, 0, 0))
    pages = _page_specs(n_pages, rows, d)
    grid_spec = pltpu.PrefetchScalarGridSpec(
        num_scalar_prefetch=1,
        grid=(b,),
        in_specs=[row_spec] * 4 + pages + pages,
        out_specs=row_spec,
        scratch_shapes=scratch,
    )
    return pl.pallas_call(
        kernel_fn,
        grid_spec=grid_spec,
        out_shape=jax.ShapeDtypeStruct((b, nh, d), F32),
        compiler_params=pltpu.CompilerParams(
            dimension_semantics=("arbitrary",), vmem_limit_bytes=VMEM_LIMIT),
        name=name,
    )(page_table, *rows4, *([k_pool] * n_pages), *([v_pool] * n_pages))


def _moba_sample_kernel(pt_ref, q_ref, kn_ref, vn_ref, g_ref, *refs, n_pages, ppb, topk, scale):
    del pt_ref
    k_refs, v_refs = refs[:n_pages], refs[n_pages:2 * n_pages]
    o_ref, s_ref = refs[2 * n_pages:]
    nh, d = q_ref.shape[1:]
    width = k_refs[0].shape[1]
    slots = width // nh
    nb = n_pages // ppb
    q = q_ref[0].astype(BF16)
    qf = q.astype(F32)
    diag = _diag_mask(nh, width)

    km = []
    for p in range(n_pages):
        kp = k_refs[p][0]
        s_ref[p] = _nt_dot(q, kp.astype(BF16)) * scale
        ksum = jnp.sum(kp.reshape(slots, nh, d), axis=0)
        if p % ppb == 0:
            km.append(ksum)
        else:
            km[-1] = km[-1] + ksum
    g = [jnp.sum(qf * (kb * (1.0 / (ppb * slots))).astype(BF16).astype(F32),
                 axis=1, keepdims=True) for kb in km]
    sel = _topk_select(g, [1.0] * nb, topk)

    s_new = jnp.sum(qf * kn_ref[0].astype(BF16).astype(F32), axis=1, keepdims=True) * scale
    m = s_new
    for p in range(n_pages):
        sp = jnp.where(jnp.logical_and(diag, sel[p // ppb] > 0.5), s_ref[p], NEG)
        s_ref[p] = sp
        m = jnp.maximum(m, jnp.max(sp, axis=1, keepdims=True))
    w_new = jnp.exp(s_new - m)
    l = w_new
    acc = w_new * vn_ref[0]
    for p in range(n_pages):
        pp = jnp.exp(s_ref[p] - m)
        l = l + jnp.sum(pp, axis=1, keepdims=True)
        acc = acc + _dot(pp.astype(BF16), v_refs[p][0].astype(BF16))
    o_ref[0] = (acc / l) * g_ref[0]


def _moba_sample(page_table, rows4, k_pool, v_pool):
    nh, d = rows4[0].shape[1:]
    n_pages = page_table.shape[1]
    rows = k_pool.shape[1]
    slots = rows // nh
    ppb = MOBA_BLOCK // slots
    assert MOBA_BLOCK % slots == 0 and n_pages % ppb == 0
    kern = functools.partial(_moba_sample_kernel, n_pages=n_pages, ppb=ppb, topk=MOBA_TOPK,
                             scale=d ** -0.5)
    return _paged_sample_call(kern, "moba_sample", page_table, rows4, k_pool, v_pool,
                              [pltpu.VMEM((n_pages, nh, rows), F32)])


def _mem_sample_kernel(q_ref, k_ref, v_ref, g_ref, o_ref, *, scale):
    s = jnp.sum(k_ref[0] * q_ref[...], axis=-1, keepdims=True) * scale
    m = jnp.max(s, axis=0, keepdims=True)
    p = jnp.exp(s - m)
    l = jnp.sum(p, axis=0)
    o_ref[0] = (jnp.sum(p * v_ref[0], axis=0) / l) * g_ref[0]


def _mem_sample(q, gate, k_cache, v_cache):
    b, nh, d = q.shape
    n_mem = k_cache.shape[1]
    row_spec = pl.BlockSpec((1, nh, d), lambda i: (i, 0, 0))
    cache_spec = pl.BlockSpec((1, n_mem, nh, d), lambda i: (i, 0, 0, 0))
    return pl.pallas_call(
        functools.partial(_mem_sample_kernel, scale=d ** -0.5),
        grid=(b,),
        in_specs=[row_spec, cache_spec, cache_spec, row_spec],
        out_specs=row_spec,
        out_shape=jax.ShapeDtypeStruct((b, nh, d), F32),
        compiler_params=pltpu.CompilerParams(
            dimension_semantics=("arbitrary",), vmem_limit_bytes=VMEM_LIMIT),
        name="mem_sample",
    )(q, k_cache, v_cache, gate)


def _merge_out_kernel(ua_ref, ub_ref, um_ref, *refs, n_halves):
    s_refs = refs[:3 * n_halves]
    x_ref, wa_ref, wb_ref, wm_ref, wo_ref, o_ref = refs[3 * n_halves:]
    tn = s_refs[0].shape[1]
    merged = []
    for n in range(n_halves):
        cols = slice(n * tn, (n + 1) * tn)
        part = s_refs[n][...] * _dot(ua_ref[...], wa_ref[:, cols])
        part = part + s_refs[n_halves + n][...] * _dot(ub_ref[...], wb_ref[:, cols])
        part = part + s_refs[2 * n_halves + n][...] * _dot(um_ref[...], wm_ref[:, cols])
        merged.append(part.astype(BF16))
    o_ref[...] = x_ref[...] + _dot(jnp.concatenate(merged, axis=1), wo_ref[...])


def _merge_out(ua, ub, um, proj, x, wa, wb, wm, wo, *, merge_seg0, tm):
    r, e = ua.shape
    d = wa.shape[1]
    tn = proj.shape[2]
    n_halves = d // tn
    assert r % tm == 0 and d % tn == 0
    act = pl.BlockSpec((tm, e), lambda i: (i, 0))
    merges = [_seg_spec((tm, tn), functools.partial(lambda i, s: (s, i, 0), s=merge_seg0 + k))
              for k in range(3 * n_halves)]
    resident = lambda shape: pl.BlockSpec(shape, lambda i: (0, 0), pipeline_mode=pl.Buffered(1))
    return pl.pallas_call(
        functools.partial(_merge_out_kernel, n_halves=n_halves),
        grid=(r // tm,),
        in_specs=[act, act, act] + merges + [
            pl.BlockSpec((tm, d), lambda i: (i, 0)),
            resident((e, d)), resident((e, d)), resident((e, d)), resident((d, d))],
        out_specs=pl.BlockSpec((tm, d), lambda i: (i, 0)),
        out_shape=jax.ShapeDtypeStruct((r, d), F32),
        compiler_params=pltpu.CompilerParams(
            dimension_semantics=("arbitrary",), vmem_limit_bytes=VMEM_LIMIT),
        name="merge_out",
    )(ua, ub, um, *([proj] * (3 * n_halves)), x, wa, wb, wm, wo)


def kernel(x_prompt, x_sample, mem_prompt, cache_sb_k, cache_sb_v, cache_moba_k, cache_moba_v,
           cache_mem_k, cache_mem_v, page_table, norm_g, w_in, qn_moba, kn_moba, qn_mem, kn_mem,
           mem_norm_g, w_mem_kv, w_br_sb, w_br_moba, w_br_mem, w_out):
    batch, seq, d_model = x_prompt.shape
    dec_batch, dec_seq, _ = x_sample.shape
    n_mem = mem_prompt.shape[1]
    n_pool, page = cache_sb_k.shape[:2]
    past_len = page_table.shape[1] * page
    sb_w, mb_w, mem_w = SB_HEADS * HEAD_DIM, MB_HEADS * HEAD_DIM, MEM_HEADS * MEM_HEAD_DIM
    assert sb_w == mb_w == mem_w == 1024 and d_model % 1024 == 0
    assert dec_seq == 1 and past_len % MOBA_BLOCK == 0 and seq % MOBA_BLOCK == 0
    tn = 1024

    names = ["qa", "ka", "va", "ga", "qb", "kb", "vb", "gb", "qm", "gm"]
    sg = {nm: i for i, nm in enumerate(names)}
    merge_seg0 = len(names)
    in_kinds = ((ID,), (ID,), (ID,), (SILU,),
                (HNORM_ROPE, HEAD_DIM, 0), (HNORM_ROPE, HEAD_DIM, 1), (ID,), (SILU,),
                (HNORM, MEM_HEAD_DIM, 2), (SILU,)) + ((SIGMOID,),) * (3 * d_model // tn)
    mem_kinds = ((HNORM, MEM_HEAD_DIM, 3), (ID,))

    pad = lambda g: jnp.pad(g, (0, MEM_HEAD_DIM - g.shape[0]))
    gains = jnp.stack([pad(qn_moba), pad(kn_moba), qn_mem, kn_mem])
    w_in_b, w_mem_b = w_in.astype(BF16), w_mem_kv.astype(BF16)
    wa, wb, wm, wo = (w.astype(BF16) for w in (w_br_sb, w_br_moba, w_br_mem, w_out))

    n_p = batch * seq
    xp = x_prompt.reshape(n_p, d_model)
    memkv = _normed_proj(mem_prompt.reshape(batch * n_mem, d_model), mem_norm_g, w_mem_b, gains,
                         mem_kinds, tm=n_mem, tn=tn, seq=n_mem, pos0=0)
    proj_p = _normed_proj(xp, norm_g, w_in_b, gains, in_kinds, tm=_row_tile(n_p, 1024), tn=tn,
                          seq=seq, pos0=0)
    ua = _sb_prompt(proj_p, batch=batch, seq=seq, heads=SB_HEADS,
                    segs=(sg["qa"], sg["ka"], sg["va"], sg["ga"]), tq=256)
    ub = _moba_prompt(proj_p, batch=batch, seq=seq, heads=MB_HEADS,
                      segs=(sg["qb"], sg["kb"], sg["vb"], sg["gb"]))
    um = _mem_prompt(proj_p, memkv, batch=batch, seq=seq, n_mem=n_mem, heads=MEM_HEADS,
                     q_seg=sg["qm"], g_seg=sg["gm"], tq=512)
    y_p = _merge_out(ua, ub, um, proj_p, xp, wa, wb, wm, wo, merge_seg0=merge_seg0,
                     tm=_row_tile(n_p, 256))

    n_s = dec_batch * dec_seq
    xs = x_sample.reshape(n_s, d_model)
    proj_s = _normed_proj(xs, norm_g, w_in_b, gains, in_kinds, tm=n_s, tn=tn, seq=dec_seq,
                          pos0=past_len)
    hseg = lambda nm, nh: proj_s[sg[nm]].reshape(n_s, nh, tn // nh)
    pool = lambda c: c.reshape(n_pool, page * c.shape[2], c.shape[3])
    rows_a = tuple(hseg(nm, SB_HEADS) for nm in ("qa", "ka", "va", "ga"))
    rows_b = tuple(hseg(nm, MB_HEADS) for nm in ("qb", "kb", "vb", "gb"))
    oa_s = _sb_sample(page_table, rows_a, pool(cache_sb_k), pool(cache_sb_v),
                      n_head=min(SB_SAMPLE_HEAD_PAGES, page_table.shape[1] - 1))
    ob_s = _moba_sample(page_table, rows_b, pool(cache_moba_k), pool(cache_moba_v))
    om_s = _mem_sample(hseg("qm", MEM_HEADS), hseg("gm", MEM_HEADS), cache_mem_k, cache_mem_v)
    us = [o.reshape(n_s, tn).astype(BF16) for o in (oa_s, ob_s, om_s)]
    y_s = _merge_out(*us, proj_s, xs, wa, wb, wm, wo, merge_seg0=merge_seg0, tm=n_s)

    kv = lambda proj, b, t: [proj[sg[nm]].reshape(b, t, SB_HEADS, HEAD_DIM)
                             for nm in ("ka", "va", "kb", "vb")]
    mk_p = memkv[0].reshape(batch, n_mem, MEM_HEADS, MEM_HEAD_DIM)
    mv_p = memkv[1].reshape(batch, n_mem, MEM_HEADS, MEM_HEAD_DIM)
    return (y_p.reshape(batch, seq, d_model), y_s.reshape(dec_batch, dec_seq, d_model),
            *kv(proj_p, batch, seq), mk_p, mv_p, *kv(proj_s, dec_batch, dec_seq))
```

```python
import functools
import math

import jax
import jax.numpy as jnp
from jax import lax
from jax.experimental import pallas as pl
from jax.experimental.pallas import tpu as pltpu

F32 = jnp.float32
BF16 = jnp.bfloat16

HEAD_DIM = 128
SB_HEADS = 8
MB_HEADS = 8
MEM_HEADS = 4
MEM_HEAD_DIM = 256
MOBA_BLOCK = 256
MOBA_TOPK = 3
ROPE_THETA = 500000.0
ROPE_DIM = HEAD_DIM // 4
EPS = 1e-6

LANES = 128
SUBLANES = 8
NEG = -1e30
SB_UNDERFLOW = -105.0
SB_SAMPLE_HEAD_PAGES = 2
SB_Q_GROUP = 8
EPILOGUE_ROWS = 256
VMEM_LIMIT = 56 * 1024 * 1024

ID, SILU, SIGMOID, HNORM_ROPE, HNORM = range(5)


def _row_tile(rows, target):
    for tile in range(min(rows, target), SUBLANES - 1, -1):
        if rows % tile == 0 and tile % SUBLANES == 0:
            return tile
    return rows


def _nt_dot(a, b):
    return lax.dot_general(a, b, (((1,), (1,)), ((), ())), preferred_element_type=F32)


def _dot(a, b):
    return jnp.dot(a, b, preferred_element_type=F32)


def _sigmoid(x):
    return 0.5 * jnp.tanh(0.5 * x) + 0.5


def _split_bf16(x, parts):
    out = []
    for _ in range(parts - 1):
        hi = x.astype(BF16)
        out.append(hi)
        x = x - hi.astype(F32)
    out.append(x.astype(BF16))
    return out


def _log_sigmoids(z):
    sp = jnp.log1p(jnp.exp(-jnp.abs(z)))
    return jnp.minimum(z, 0.0) - sp, jnp.minimum(-z, 0.0) - sp


def _seg_spec(block, index):
    return pl.BlockSpec((None,) + block, index)


def _rope_angles(pos, half, theta):
    lane = lax.broadcasted_iota(jnp.int32, (1, LANES), 1)
    inv_freq = jnp.exp((lane % half).astype(F32) * (-math.log(theta) / half))
    ang = pos * inv_freq
    return jnp.cos(ang), jnp.sin(ang)


def _proj_kernel(x_ref, g_ref, w_ref, gains_ref, o_ref, h_ref, rope_ref, *, kinds, tm, seq, pos0):
    i = pl.program_id(0)
    j = pl.program_id(1)
    half = ROPE_DIM // 2

    @pl.when(j == 0)
    def _():
        x = x_ref[...]
        ms = jnp.mean(x * x, axis=-1, keepdims=True)
        h_ref[...] = ((x * lax.rsqrt(ms + EPS)) * g_ref[...]).astype(BF16)

    def tiles_of(kind_pred):
        cond = None
        for c, kd in enumerate(kinds):
            if kind_pred(kd):
                t = j == c
                cond = t if cond is None else jnp.logical_or(cond, t)
        return cond

    chunk = _row_tile(tm, EPILOGUE_ROWS)
    chunks = [slice(r, r + chunk) for r in range(0, tm, chunk)]

    def matmul(rows=slice(None)):
        return _dot(h_ref[rows, :], w_ref[...])

    cond = tiles_of(lambda kd: kd[0] == ID)
    if cond is not None:
        @pl.when(cond)
        def _():
            o_ref[...] = matmul()

    cond = tiles_of(lambda kd: kd[0] == SILU)
    if cond is not None:
        @pl.when(cond)
        def _():
            for rows in chunks:
                a = matmul(rows)
                o_ref[rows, :] = a * _sigmoid(a)

    cond = tiles_of(lambda kd: kd[0] == SIGMOID)
    if cond is not None:
        @pl.when(cond)
        def _():
            for rows in chunks:
                o_ref[rows, :] = _sigmoid(matmul(rows))

    rope_tiles = [c for c, kd in enumerate(kinds) if kd[0] == HNORM_ROPE]
    if rope_tiles:
        assert seq % tm == 0 or tm % seq == 0
        local_period = tm if seq % tm == 0 else seq

        @pl.when(jnp.logical_and(i == 0, j == rope_tiles[0]))
        def _():
            r = lax.broadcasted_iota(jnp.int32, (tm, 1), 0)
            cos, sin = _rope_angles(lax.rem(r, local_period).astype(F32), half, ROPE_THETA)
            rope_ref[0] = cos
            rope_ref[1] = sin

    for c, kd in enumerate(kinds):
        if kd[0] not in (HNORM_ROPE, HNORM):
            continue

        @pl.when(j == c)
        def _(kd=kd):
            hd = kd[1]
            gain = gains_ref[kd[2]:kd[2] + 1, :hd]
            if kd[0] == HNORM_ROPE:
                base = pos0 + (lax.rem(i * tm, seq) if seq % tm == 0 else 0 * i)
                cos_b, sin_b = _rope_angles(base.astype(F32), half, ROPE_THETA)
                lane = lax.broadcasted_iota(jnp.int32, (1, LANES), 1)
                in_rope = lane < 2 * half
            for rows in chunks:
                a = matmul(rows)
                if kd[0] == HNORM_ROPE:
                    cos_l, sin_l = rope_ref[0, rows, :], rope_ref[1, rows, :]
                    cos = jnp.where(in_rope, cos_l * cos_b - sin_l * sin_b, 1.0)
                    sin = jnp.where(in_rope, sin_l * cos_b + cos_l * sin_b, 0.0)
                    s_lo = jnp.where(lane < half, -sin, 0.0)
                    s_hi = jnp.where(lane >= half, sin, 0.0)
                for h in range(a.shape[1] // hd):
                    ah = a[:, h * hd:(h + 1) * hd]
                    ms = jnp.mean(ah * ah, axis=-1, keepdims=True)
                    ah = (ah * lax.rsqrt(ms + EPS)) * gain
                    if kd[0] == HNORM_ROPE:
                        ah = (ah * cos + pltpu.roll(ah, LANES - half, 1) * s_lo
                              + pltpu.roll(ah, half, 1) * s_hi)
                    o_ref[rows, h * hd:(h + 1) * hd] = ah


def _normed_proj(x, norm_g, w_bf16, gains, kinds, *, tm, tn, seq, pos0):
    r, d = x.shape
    n = w_bf16.shape[1]
    assert r % tm == 0 and n % tn == 0 and len(kinds) == n // tn
    return pl.pallas_call(
        functools.partial(_proj_kernel, kinds=kinds, tm=tm, seq=seq, pos0=pos0),
        grid=(r // tm, n // tn),
        in_specs=[
            pl.BlockSpec((tm, d), lambda i, j: (i, 0)),
            pl.BlockSpec((1, d), lambda i, j: (0, 0)),
            pl.BlockSpec((d, tn), lambda i, j: (0, j)),
            pl.BlockSpec(gains.shape, lambda i, j: (0, 0)),
        ],
        out_specs=_seg_spec((tm, tn), lambda i, j: (j, i, 0)),
        out_shape=jax.ShapeDtypeStruct((n // tn, r, tn), F32),
        scratch_shapes=[pltpu.VMEM((tm, d), BF16), pltpu.VMEM((2, tm, LANES), F32)],
        compiler_params=pltpu.CompilerParams(
            dimension_semantics=("arbitrary", "arbitrary"), vmem_limit_bytes=VMEM_LIMIT),
        name="normed_proj",
    )(x, norm_g.reshape(1, d), w_bf16, gains)


def _sb_prompt_kernel(q_ref, k_ref, v_ref, g_ref, o_ref, *, tq, scale):
    seq = q_ref.shape[0]
    row = lax.broadcasted_iota(jnp.int32, (tq, tq), 0)
    col = lax.broadcasted_iota(jnp.int32, (tq, tq), 1)
    later = (row > col).astype(BF16)
    strict = col < row

    def log_terms(qb, j):
        start = pl.multiple_of(j * tq, tq)
        return _log_sigmoids(_nt_dot(qb, k_ref[pl.ds(start, tq), :].astype(BF16)) * scale)

    def weights(ls_pos, l_neg, c):
        l_hi, l_lo = _split_bf16(l_neg, 2)
        return jnp.exp(ls_pos + _dot(l_hi, later) + _dot(l_lo, later) + c)

    def values(a, j):
        start = pl.multiple_of(j * tq, tq)
        return _dot(a.astype(BF16), v_ref[pl.ds(start, tq), :].astype(BF16))

    def newest_tiles(qi):
        qb = q_ref[pl.ds(pl.multiple_of(qi * tq, tq), tq), :].astype(BF16)
        ls_pos, ls_neg = log_terms(qb, qi)
        l_neg = jnp.where(strict, ls_neg, 0.0)
        acc = values(jnp.where(strict, weights(ls_pos, l_neg, 0.0), 0.0), qi)
        c = jnp.sum(l_neg, axis=1, keepdims=True)
        has_prev = (jnp.zeros_like(row) + qi) > 0
        jp = jnp.maximum(qi - 1, 0)
        ls_pos, ls_neg = log_terms(qb, jp)
        l_neg = jnp.where(has_prev, ls_neg, 0.0)
        acc = acc + values(jnp.where(has_prev, weights(ls_pos, l_neg, c), 0.0), jp)
        return qb, c + jnp.sum(l_neg, axis=1, keepdims=True), acc

    def older_tiles(qi, qb, c, acc):
        def more(state):
            j, c_max, _, _ = state
            return jnp.logical_and(j >= 0, c_max > SB_UNDERFLOW)

        def tile(state):
            j, _, c, acc = state
            ls_pos, l_neg = log_terms(qb, j)
            acc = acc + values(weights(ls_pos, l_neg, c), j)
            c = c + jnp.sum(l_neg, axis=1, keepdims=True)
            return j - 1, jnp.max(c), c, acc

        return lax.while_loop(more, tile, (qi - 2, jnp.max(c), c, acc))[3]

    n_tiles = seq // tq
    group = max(g for g in range(1, SB_Q_GROUP + 1) if n_tiles % g == 0)

    def q_group(gi, carry):
        tiles = [gi * group + u for u in range(group)]
        newest = [newest_tiles(qi) for qi in tiles]
        for qi, (qb, c, acc) in zip(tiles, newest):
            acc = older_tiles(qi, qb, c, acc)
            rows = pl.ds(pl.multiple_of(qi * tq, tq), tq)
            o_ref[rows, :] = (acc * g_ref[rows, :]).astype(o_ref.dtype)
        return carry

    lax.fori_loop(0, n_tiles // group, q_group, 0)


def _head_specs(seq, d, segs):
    return [_seg_spec((seq, d), functools.partial(lambda b, h, s: (s, b, h), s=s)) for s in segs]


def _sb_prompt(proj, *, batch, seq, heads, segs, tq):
    d = HEAD_DIM
    return pl.pallas_call(
        functools.partial(_sb_prompt_kernel, tq=tq, scale=d ** -0.5),
        grid=(batch, heads),
        in_specs=_head_specs(seq, d, segs),
        out_specs=pl.BlockSpec((seq, d), lambda b, h: (b, h)),
        out_shape=jax.ShapeDtypeStruct((batch * seq, heads * d), BF16),
        compiler_params=pltpu.CompilerParams(
            dimension_semantics=("arbitrary", "arbitrary"), vmem_limit_bytes=VMEM_LIMIT),
        name="sb_prompt",
    )(proj, proj, proj, proj)


def _topk_select(g, valid, topk):
    nb = len(g)
    sel = []
    for m in range(nb):
        rank = jnp.zeros_like(g[m])
        for o in range(nb):
            if o == m:
                continue
            beats = g[o] >= g[m] if o < m else g[o] > g[m]
            rank = rank + jnp.where(beats, valid[o], 0.0)
        sel.append(jnp.where(rank < topk, valid[m], 0.0))
    return sel


def _moba_prompt_kernel(q_ref, k_ref, v_ref, g_ref, o_ref, *, blk, topk, scale):
    seq = q_ref.shape[0]
    nb = seq // blk
    row = lax.broadcasted_iota(jnp.int32, (blk, blk), 0)
    col = lax.broadcasted_iota(jnp.int32, (blk, blk), 1)
    causal = col <= row
    eye = (row == col).astype(BF16)
    km = jnp.concatenate(
        [jnp.mean(k_ref[n * blk:(n + 1) * blk, :], axis=0, keepdims=True) for n in range(nb)],
        axis=0).astype(BF16)
    v1 = jnp.concatenate([v_ref[...].astype(BF16), jnp.ones(v_ref.shape, BF16)], axis=1)

    for i in range(nb):
        q = q_ref[i * blk:(i + 1) * blk, :].astype(BF16)
        width = (i + 1) * blk
        s = _nt_dot(q, k_ref[0:width, :].astype(BF16)) * scale
        tiles = [s[:, n * blk:(n + 1) * blk] for n in range(i + 1)]
        if i > topk:
            g = _nt_dot(km, q)
            sel_t = jnp.concatenate(
                _topk_select([g[n:n + 1, :] for n in range(i)], [1.0] * i, topk)
                + [jnp.zeros((nb - i, blk), F32)], axis=0)
            sel_c = _nt_dot(eye, sel_t.astype(BF16))
            for n in range(i):
                tiles[n] = jnp.where(sel_c[:, n:n + 1] > 0.5, tiles[n], NEG)
        tiles[i] = jnp.where(causal, tiles[i], NEG)
        m = functools.reduce(jnp.maximum, [t[:, k:k + LANES] for t in tiles
                                           for k in range(0, blk, LANES)])
        m = m.max(axis=1, keepdims=True)
        p = jnp.concatenate([jnp.exp(t - m).astype(BF16) for t in tiles], axis=1)
        pv = _dot(p, v1[0:width, :])
        d = v_ref.shape[1]
        o_ref[i * blk:(i + 1) * blk, :] = (
            (pv[:, :d] / pv[:, d:d + 1]) * g_ref[i * blk:(i + 1) * blk, :]).astype(o_ref.dtype)


def _moba_prompt(proj, *, batch, seq, heads, segs):
    d = HEAD_DIM
    return pl.pallas_call(
        functools.partial(_moba_prompt_kernel, blk=MOBA_BLOCK, topk=MOBA_TOPK, scale=d ** -0.5),
        grid=(batch, heads),
        in_specs=_head_specs(seq, d, segs),
        out_specs=pl.BlockSpec((seq, d), lambda b, h: (b, h)),
        out_shape=jax.ShapeDtypeStruct((batch * seq, heads * d), BF16),
        compiler_params=pltpu.CompilerParams(
            dimension_semantics=("arbitrary", "arbitrary"), vmem_limit_bytes=VMEM_LIMIT),
        name="moba_prompt",
    )(proj, proj, proj, proj)


def _mem_prompt_kernel(q_ref, k_ref, v_ref, g_ref, o_ref, *, heads, scale):
    d = q_ref.shape[1] // heads
    for h in range(heads):
        cols = slice(h * d, (h + 1) * d)
        s = _nt_dot(q_ref[:, cols].astype(BF16), k_ref[:, cols].astype(BF16)) * scale
        m = jnp.max(s, axis=1, keepdims=True)
        p = jnp.exp(s - m)
        l = jnp.sum(p, axis=1, keepdims=True)
        o = _dot(p.astype(BF16), v_ref[:, cols].astype(BF16)) / l
        o_ref[:, cols] = (o * g_ref[:, cols]).astype(o_ref.dtype)


def _mem_prompt(proj, memkv, *, batch, seq, n_mem, heads, q_seg, g_seg, tq):
    w = heads * MEM_HEAD_DIM
    nq = seq // tq
    return pl.pallas_call(
        functools.partial(_mem_prompt_kernel, heads=heads, scale=MEM_HEAD_DIM ** -0.5),
        grid=(batch, nq),
        in_specs=[
            _seg_spec((tq, w), lambda b, i: (q_seg, b * nq + i, 0)),
            _seg_spec((n_mem, w), lambda b, i: (0, b, 0)),
            _seg_spec((n_mem, w), lambda b, i: (1, b, 0)),
            _seg_spec((tq, w), lambda b, i: (g_seg, b * nq + i, 0)),
        ],
        out_specs=pl.BlockSpec((tq, w), lambda b, i: (b * nq + i, 0)),
        out_shape=jax.ShapeDtypeStruct((batch * seq, w), BF16),
        compiler_params=pltpu.CompilerParams(
            dimension_semantics=("arbitrary", "arbitrary"), vmem_limit_bytes=VMEM_LIMIT),
        name="mem_prompt",
    )(proj, memkv, memkv, proj)


def _diag_mask(nh, width):
    sub = lax.broadcasted_iota(jnp.int32, (nh, width), 0)
    lane = lax.broadcasted_iota(jnp.int32, (nh, width), 1)
    return (lane % nh) == sub


def _class_scan(x, nh, cyclic):
    width = x.shape[-1]
    lane = lax.broadcasted_iota(jnp.int32, x.shape, x.ndim - 1)
    shift = nh
    while shift < width:
        nxt = pltpu.roll(x, width - shift, x.ndim - 1)
        x = x + (nxt if cyclic else jnp.where(lane + shift < width, nxt, 0.0))
        shift *= 2
    return x


def _page_specs(n_pages, rows, d):
    def spec(p):
        return pl.BlockSpec((1, rows, d), lambda b, pt: (pt[b, p], 0, 0))
    return [spec(p) for p in range(n_pages)]


def _sb_pages(q, k_refs, v_refs, z_ref, first_page, c_after, acc, *, scale, q_pos):
    nh = q.shape[0]
    n = len(k_refs)
    width = k_refs[0].shape[1]
    slots = width // nh
    diag = _diag_mask(nh, width)
    for p in range(n):
        z_ref[p:p + 1, :] = jnp.sum(
            jnp.where(diag, _nt_dot(q, k_refs[p][0].astype(BF16)), 0.0), axis=0, keepdims=True)
    z = z_ref[0:n, :] * scale
    prow = lax.broadcasted_iota(jnp.int32, (n, width), 0)
    plane = lax.broadcasted_iota(jnp.int32, (n, width), 1)
    mask = ((first_page + prow) * slots + plane // nh) < q_pos
    ls_pos, ls_neg = _log_sigmoids(z)
    l_neg = jnp.where(mask, ls_neg, 0.0)
    within = _class_scan(l_neg, nh, cyclic=False) - l_neg
    totals = _class_scan(l_neg, nh, cyclic=True)
    pr = lax.broadcasted_iota(jnp.int32, (n, n), 0)
    pc = lax.broadcasted_iota(jnp.int32, (n, n), 1)
    later = (pc > pr).astype(BF16)
    carry = sum(_dot(later, part) for part in _split_bf16(totals, 3))
    a = jnp.where(mask, jnp.exp(ls_pos + within + carry + c_after), 0.0)
    for p in range(n):
        w_p = jnp.where(diag, jnp.broadcast_to(a[p:p + 1, :], (nh, width)), 0.0).astype(BF16)
        acc = acc + _dot(w_p, v_refs[p][0].astype(BF16))
    return acc, c_after + jnp.sum(totals, axis=0, keepdims=True)


def _sb_sample_head_kernel(pt_ref, q_ref, kn_ref, vn_ref, g_ref, *refs, n_head, first_page, scale,
                           past_len):
    del pt_ref
    k_refs, v_refs = refs[:n_head], refs[n_head:2 * n_head]
    o_ref, acc_ref, c_ref, more_ref, z_ref = refs[2 * n_head:]
    nh, d = q_ref.shape[1:]
    width = k_refs[0].shape[1]
    q = q_ref[0].astype(BF16)
    q_pos = past_len
    diag = _diag_mask(nh, LANES)
    lane = lax.broadcasted_iota(jnp.int32, (1, width), 1)

    kn = jnp.concatenate([kn_ref[0], jnp.zeros((LANES - nh, d), F32)], axis=0).astype(BF16)
    vn = jnp.concatenate([vn_ref[0], jnp.zeros((LANES - nh, d), F32)], axis=0).astype(BF16)
    z_n = jnp.sum(jnp.where(diag, _nt_dot(q, kn), 0.0), axis=0, keepdims=True)
    z_n = jnp.concatenate([z_n, jnp.zeros((1, width - LANES), F32)], axis=1) * scale
    mask_n = (past_len + lane // nh) < q_pos
    lsp_n, lsn_n = _log_sigmoids(z_n)
    a_n = jnp.where(mask_n, jnp.exp(lsp_n), 0.0)
    c_new = _class_scan(jnp.where(mask_n, lsn_n, 0.0), nh, cyclic=True)
    w_n = jnp.where(diag, jnp.broadcast_to(a_n[:, :LANES], (nh, LANES)), 0.0).astype(BF16)

    acc, c = _sb_pages(q, k_refs, v_refs, z_ref, first_page, c_new, _dot(w_n, vn),
                       scale=scale, q_pos=q_pos)
    o_ref[0] = acc * g_ref[0]
    acc_ref[0] = acc
    c_ref[0] = c
    more_ref[0] = jnp.where(jnp.max(c) > SB_UNDERFLOW, 1, 0) + jnp.zeros((1, LANES), jnp.int32)


def _sb_sample_tail_kernel(pt_ref, more_ref, src_ref, q_ref, g_ref, acc_ref, c_ref, *refs,
                           n_tail, scale, past_len):
    del pt_ref, src_ref
    k_refs, v_refs = refs[:n_tail], refs[n_tail:2 * n_tail]
    o_ref, z_ref = refs[2 * n_tail:]
    b = pl.program_id(0)

    @pl.when(more_ref[b] == 0)
    def _():
        o_ref[0] = acc_ref[0] * g_ref[0]

    @pl.when(more_ref[b] != 0)
    def _():
        acc, _ = _sb_pages(q_ref[0].astype(BF16), k_refs, v_refs, z_ref, 0, c_ref[0], acc_ref[0],
                           scale=scale, q_pos=past_len)
        o_ref[0] = acc * g_ref[0]


def _sb_sample(page_table, rows4, k_pool, v_pool, *, n_head):
    q, k_new, v_new, gate = rows4
    b, nh, d = q.shape
    n_pages = page_table.shape[1]
    rows = k_pool.shape[1]
    n_tail = n_pages - n_head
    past_len = n_pages * (rows // nh)
    scale = d ** -0.5
    params = pltpu.CompilerParams(dimension_semantics=("arbitrary",), vmem_limit_bytes=VMEM_LIMIT)

    row_spec = pl.BlockSpec((1, nh, d), lambda i, pt: (i, 0, 0))
    head_pages = [pl.BlockSpec((1, rows, d), functools.partial(
        lambda i, pt, p: (pt[i, p], 0, 0), p=n_tail + p)) for p in range(n_head)]
    out, acc, c, more = pl.pallas_call(
        functools.partial(_sb_sample_head_kernel, n_head=n_head, first_page=n_tail, scale=scale,
                          past_len=past_len),
        grid_spec=pltpu.PrefetchScalarGridSpec(
            num_scalar_prefetch=1,
            grid=(b,),
            in_specs=[row_spec] * 4 + head_pages + head_pages,
            out_specs=[row_spec, row_spec, pl.BlockSpec((1, 1, rows), lambda i, pt: (i, 0, 0)),
                       pl.BlockSpec((1, 1, LANES), lambda i, pt: (i, 0, 0))],
            scratch_shapes=[pltpu.VMEM((n_head, rows), F32)],
        ),
        out_shape=[jax.ShapeDtypeStruct((b, nh, d), F32), jax.ShapeDtypeStruct((b, nh, d), F32),
                   jax.ShapeDtypeStruct((b, 1, rows), F32),
                   jax.ShapeDtypeStruct((b, 1, LANES), jnp.int32)],
        compiler_params=params,
        name="sb_sample_head",
    )(page_table, q, k_new, v_new, gate, *([k_pool] * n_head), *([v_pool] * n_head))
    more = more[:, 0, 0]

    def tail():
        src = lax.cummax(jnp.where(more > 0, jnp.arange(b, dtype=jnp.int32), 0), axis=0)
        row3 = pl.BlockSpec((1, nh, d), lambda i, pt, mo, sr: (i, 0, 0))

        def tail_page(i, pt, mo, sr, p):
            seq_row = jnp.clip(sr[jnp.minimum(i, b - 1)], 0, b - 1)
            return (pt[seq_row, p], 0, 0)

        tail_pages = [pl.BlockSpec((1, rows, d), functools.partial(tail_page, p=p))
                      for p in range(n_tail)]
        return pl.pallas_call(
            functools.partial(_sb_sample_tail_kernel, n_tail=n_tail, scale=scale,
                              past_len=past_len),
            grid_spec=pltpu.PrefetchScalarGridSpec(
                num_scalar_prefetch=3,
                grid=(b,),
                in_specs=[row3, row3, row3,
                          pl.BlockSpec((1, 1, rows), lambda i, pt, mo, sr: (i, 0, 0))]
                + tail_pages + tail_pages,
                out_specs=row3,
                scratch_shapes=[pltpu.VMEM((n_tail, rows), F32)],
            ),
            out_shape=jax.ShapeDtypeStruct((b, nh, d), F32),
            compiler_params=params,
            name="sb_sample_tail",
        )(page_table, more, src, q, gate, acc, c, *([k_pool] * n_tail), *([v_pool] * n_tail))

    return lax.cond(jnp.max(more) > 0, tail, lambda: out)


def _paged_sample_call(kernel_fn, name, page_table, rows4, k_pool, v_pool, scratch):
    b, nh, d = rows4[0].shape
    n_pages = page_table.shape[1]
    rows = k_pool.shape[1]
    row_spec = pl.BlockSpec((1, nh, d), lambda i, pt: (i, 0, 0))
    pages = _page_specs(n_pages, rows, d)
    grid_spec = pltpu.PrefetchScalarGridSpec(
        num_scalar_prefetch=1,
        grid=(b,),
        in_specs=[row_spec] * 4 + pages + pages,
        out_specs=row_spec,
        scratch_shapes=scratch,
    )
    return pl.pallas_call(
        kernel_fn,
        grid_spec=grid_spec,
        out_shape=jax.ShapeDtypeStruct((b, nh, d), F32),
        compiler_params=pltpu.CompilerParams(
            dimension_semantics=("arbitrary",), vmem_limit_bytes=VMEM_LIMIT),
        name=name,
    )(page_table, *rows4, *([k_pool] * n_pages), *([v_pool] * n_pages))


def _moba_sample_kernel(pt_ref, q_ref, kn_ref, vn_ref, g_ref, *refs, n_pages, ppb, topk, scale):
    del pt_ref
    k_refs, v_refs = refs[:n_pages], refs[n_pages:2 * n_pages]
    o_ref, s_ref = refs[2 * n_pages:]
    nh, d = q_ref.shape[1:]
    width = k_refs[0].shape[1]
    slots = width // nh
    nb = n_pages // ppb
    q = q_ref[0].astype(BF16)
    qf = q.astype(F32)
    diag = _diag_mask(nh, width)

    km = []
    for p in range(n_pages):
        kp = k_refs[p][0]
        s_ref[p] = _nt_dot(q, kp.astype(BF16)) * scale
        ksum = jnp.sum(kp.reshape(slots, nh, d), axis=0)
        if p % ppb == 0:
            km.append(ksum)
        else:
            km[-1] = km[-1] + ksum
    g = [jnp.sum(qf * (kb * (1.0 / (ppb * slots))).astype(BF16).astype(F32),
                 axis=1, keepdims=True) for kb in km]
    sel = _topk_select(g, [1.0] * nb, topk)

    s_new = jnp.sum(qf * kn_ref[0].astype(BF16).astype(F32), axis=1, keepdims=True) * scale
    m = s_new
    for p in range(n_pages):
        sp = jnp.where(jnp.logical_and(diag, sel[p // ppb] > 0.5), s_ref[p], NEG)
        s_ref[p] = sp
        m = jnp.maximum(m, jnp.max(sp, axis=1, keepdims=True))
    w_new = jnp.exp(s_new - m)
    l = w_new
    acc = w_new * vn_ref[0]
    for p in range(n_pages):
        pp = jnp.exp(s_ref[p] - m)
        l = l + jnp.sum(pp, axis=1, keepdims=True)
        acc = acc + _dot(pp.astype(BF16), v_refs[p][0].astype(BF16))
    o_ref[0] = (acc / l) * g_ref[0]


def _moba_sample(page_table, rows4, k_pool, v_pool):
    nh, d = rows4[0].shape[1:]
    n_pages = page_table.shape[1]
    rows = k_pool.shape[1]
    slots = rows // nh
    ppb = MOBA_BLOCK // slots
    assert MOBA_BLOCK % slots == 0 and n_pages % ppb == 0
    kern = functools.partial(_moba_sample_kernel, n_pages=n_pages, ppb=ppb, topk=MOBA_TOPK,
                             scale=d ** -0.5)
    return _paged_sample_call(kern, "moba_sample", page_table, rows4, k_pool, v_pool,
                              [pltpu.VMEM((n_pages, nh, rows), F32)])


def _mem_sample_kernel(q_ref, k_ref, v_ref, g_ref, o_ref, *, scale):
    s = jnp.sum(k_ref[0] * q_ref[...], axis=-1, keepdims=True) * scale
    m = jnp.max(s, axis=0, keepdims=True)
    p = jnp.exp(s - m)
    l = jnp.sum(p, axis=0)
    o_ref[0] = (jnp.sum(p * v_ref[0], axis=0) / l) * g_ref[0]


def _mem_sample(q, gate, k_cache, v_cache):
    b, nh, d = q.shape
    n_mem = k_cache.shape[1]
    row_spec = pl.BlockSpec((1, nh, d), lambda i: (i, 0, 0))
    cache_spec = pl.BlockSpec((1, n_mem, nh, d), lambda i: (i, 0, 0, 0))
    return pl.pallas_call(
        functools.partial(_mem_sample_kernel, scale=d ** -0.5),
        grid=(b,),
        in_specs=[row_spec, cache_spec, cache_spec, row_spec],
        out_specs=row_spec,
        out_shape=jax.ShapeDtypeStruct((b, nh, d), F32),
        compiler_params=pltpu.CompilerParams(
            dimension_semantics=("arbitrary",), vmem_limit_bytes=VMEM_LIMIT),
        name="mem_sample",
    )(q, k_cache, v_cache, gate)


def _merge_out_kernel(ua_ref, ub_ref, um_ref, *refs, n_halves):
    s_refs = refs[:3 * n_halves]
    x_ref, wa_ref, wb_ref, wm_ref, wo_ref, o_ref = refs[3 * n_halves:]
    tn = s_refs[0].shape[1]
    merged = []
    for n in range(n_halves):
        cols = slice(n * tn, (n + 1) * tn)
        part = s_refs[n][...] * _dot(ua_ref[...], wa_ref[:, cols])
        part = part + s_refs[n_halves + n][...] * _dot(ub_ref[...], wb_ref[:, cols])
        part = part + s_refs[2 * n_halves + n][...] * _dot(um_ref[...], wm_ref[:, cols])
        merged.append(part.astype(BF16))
    o_ref[...] = x_ref[...] + _dot(jnp.concatenate(merged, axis=1), wo_ref[...])


def _merge_out(ua, ub, um, proj, x, wa, wb, wm, wo, *, merge_seg0, tm):
    r, e = ua.shape
    d = wa.shape[1]
    tn = proj.shape[2]
    n_halves = d // tn
    assert r % tm == 0 and d % tn == 0
    act = pl.BlockSpec((tm, e), lambda i: (i, 0))
    merges = [_seg_spec((tm, tn), functools.partial(lambda i, s: (s, i, 0), s=merge_seg0 + k))
              for k in range(3 * n_halves)]
    resident = lambda shape: pl.BlockSpec(shape, lambda i: (0, 0), pipeline_mode=pl.Buffered(1))
    return pl.pallas_call(
        functools.partial(_merge_out_kernel, n_halves=n_halves),
        grid=(r // tm,),
        in_specs=[act, act, act] + merges + [
            pl.BlockSpec((tm, d), lambda i: (i, 0)),
            resident((e, d)), resident((e, d)), resident((e, d)), resident((d, d))],
        out_specs=pl.BlockSpec((tm, d), lambda i: (i, 0)),
        out_shape=jax.ShapeDtypeStruct((r, d), F32),
        compiler_params=pltpu.CompilerParams(
            dimension_semantics=("arbitrary",), vmem_limit_bytes=VMEM_LIMIT),
        name="merge_out",
    )(ua, ub, um, *([proj] * (3 * n_halves)), x, wa, wb, wm, wo)


def kernel(x_prompt, x_sample, mem_prompt, cache_sb_k, cache_sb_v, cache_moba_k, cache_moba_v,
           cache_mem_k, cache_mem_v, page_table, norm_g, w_in, qn_moba, kn_moba, qn_mem, kn_mem,
           mem_norm_g, w_mem_kv, w_br_sb, w_br_moba, w_br_mem, w_out):
    batch, seq, d_model = x_prompt.shape
    dec_batch, dec_seq, _ = x_sample.shape
    n_mem = mem_prompt.shape[1]
    n_pool, page = cache_sb_k.shape[:2]
    past_len = page_table.shape[1] * page
    sb_w, mb_w, mem_w = SB_HEADS * HEAD_DIM, MB_HEADS * HEAD_DIM, MEM_HEADS * MEM_HEAD_DIM
    assert sb_w == mb_w == mem_w == 1024 and d_model % 1024 == 0
    assert dec_seq == 1 and past_len % MOBA_BLOCK == 0 and seq % MOBA_BLOCK == 0
    tn = 1024

    names = ["qa", "ka", "va", "ga", "qb", "kb", "vb", "gb", "qm", "gm"]
    sg = {nm: i for i, nm in enumerate(names)}
    merge_seg0 = len(names)
    in_kinds = ((ID,), (ID,), (ID,), (SILU,),
                (HNORM_ROPE, HEAD_DIM, 0), (HNORM_ROPE, HEAD_DIM, 1), (ID,), (SILU,),
                (HNORM, MEM_HEAD_DIM, 2), (SILU,)) + ((SIGMOID,),) * (3 * d_model // tn)
    mem_kinds = ((HNORM, MEM_HEAD_DIM, 3), (ID,))

    pad = lambda g: jnp.pad(g, (0, MEM_HEAD_DIM - g.shape[0]))
    gains = jnp.stack([pad(qn_moba), pad(kn_moba), qn_mem, kn_mem])
    w_in_b, w_mem_b = w_in.astype(BF16), w_mem_kv.astype(BF16)
    wa, wb, wm, wo = (w.astype(BF16) for w in (w_br_sb, w_br_moba, w_br_mem, w_out))

    n_p = batch * seq
    xp = x_prompt.reshape(n_p, d_model)
    memkv = _normed_proj(mem_prompt.reshape(batch * n_mem, d_model), mem_norm_g, w_mem_b, gains,
                         mem_kinds, tm=n_mem, tn=tn, seq=n_mem, pos0=0)
    proj_p = _normed_proj(xp, norm_g, w_in_b, gains, in_kinds, tm=_row_tile(n_p, 1024), tn=tn,
                          seq=seq, pos0=0)
    ua = _sb_prompt(proj_p, batch=batch, seq=seq, heads=SB_HEADS,
                    segs=(sg["qa"], sg["ka"], sg["va"], sg["ga"]), tq=256)
    ub = _moba_prompt(proj_p, batch=batch, seq=seq, heads=MB_HEADS,
                      segs=(sg["qb"], sg["kb"], sg["vb"], sg["gb"]))
    um = _mem_prompt(proj_p, memkv, batch=batch, seq=seq, n_mem=n_mem, heads=MEM_HEADS,
                     q_seg=sg["qm"], g_seg=sg["gm"], tq=512)
    y_p = _merge_out(ua, ub, um, proj_p, xp, wa, wb, wm, wo, merge_seg0=merge_seg0,
                     tm=_row_tile(n_p, 256))

    n_s = dec_batch * dec_seq
    xs = x_sample.reshape(n_s, d_model)
    proj_s = _normed_proj(xs, norm_g, w_in_b, gains, in_kinds, tm=n_s, tn=tn, seq=dec_seq,
                          pos0=past_len)
    hseg = lambda nm, nh: proj_s[sg[nm]].reshape(n_s, nh, tn // nh)
    pool = lambda c: c.reshape(n_pool, page * c.shape[2], c.shape[3])
    rows_a = tuple(hseg(nm, SB_HEADS) for nm in ("qa", "ka", "va", "ga"))
    rows_b = tuple(hseg(nm, MB_HEADS) for nm in ("qb", "kb", "vb", "gb"))
    oa_s = _sb_sample(page_table, rows_a, pool(cache_sb_k), pool(cache_sb_v),
                      n_head=min(SB_SAMPLE_HEAD_PAGES, page_table.shape[1] - 1))
    ob_s = _moba_sample(page_table, rows_b, pool(cache_moba_k), pool(cache_moba_v))
    om_s = _mem_sample(hseg("qm", MEM_HEADS), hseg("gm", MEM_HEADS), cache_mem_k, cache_mem_v)
    us = [o.reshape(n_s, tn).astype(BF16) for o in (oa_s, ob_s, om_s)]
    y_s = _merge_out(*us, proj_s, xs, wa, wb, wm, wo, merge_seg0=merge_seg0, tm=n_s)

    kv = lambda proj, b, t: [proj[sg[nm]].reshape(b, t, SB_HEADS, HEAD_DIM)
                             for nm in ("ka", "va", "kb", "vb")]
    mk_p = memkv[0].reshape(batch, n_mem, MEM_HEADS, MEM_HEAD_DIM)
    mv_p = memkv[1].reshape(batch, n_mem, MEM_HEADS, MEM_HEAD_DIM)
    return (y_p.reshape(batch, seq, d_model), y_s.reshape(dec_batch, dec_seq, d_model),
            *kv(proj_p, batch, seq), mk_p, mv_p, *kv(proj_s, dec_batch, dec_seq))
```

```python
import functools
import math

import jax
import jax.numpy as jnp
from jax import lax
from jax.experimental import pallas as pl
from jax.experimental.pallas import tpu as pltpu

F32 = jnp.float32
BF16 = jnp.bfloat16

HEAD_DIM = 128
SB_HEADS = 8
MB_HEADS = 8
MEM_HEADS = 4
MEM_HEAD_DIM = 256
MOBA_BLOCK = 256
MOBA_TOPK = 3
ROPE_THETA = 500000.0
ROPE_DIM = HEAD_DIM // 4
EPS = 1e-6

LANES = 128
SUBLANES = 8
NEG = -1e30
SB_UNDERFLOW = -105.0
SB_SAMPLE_HEAD_PAGES = 2
SB_Q_GROUP = 8
EPILOGUE_ROWS = 256
VMEM_LIMIT = 56 * 1024 * 1024

ID, SILU, SIGMOID, HNORM_ROPE, HNORM = range(5)


def _row_tile(rows, target):
    for tile in range(min(rows, target), SUBLANES - 1, -1):
        if rows % tile == 0 and tile % SUBLANES == 0:
            return tile
    return rows


def _nt_dot(a, b):
    return lax.dot_general(a, b, (((1,), (1,)), ((), ())), preferred_element_type=F32)


def _dot(a, b):
    return jnp.dot(a, b, preferred_element_type=F32)


def _sigmoid(x):
    return 0.5 * jnp.tanh(0.5 * x) + 0.5


def _split_bf16(x, parts):
    out = []
    for _ in range(parts - 1):
        hi = x.astype(BF16)
        out.append(hi)
        x = x - hi.astype(F32)
    out.append(x.astype(BF16))
    return out


def _log_sigmoids(z):
    sp = jnp.log1p(jnp.exp(-jnp.abs(z)))
    return jnp.minimum(z, 0.0) - sp, jnp.minimum(-z, 0.0) - sp


def _seg_spec(block, index):
    return pl.BlockSpec((None,) + block, index)


def _rope_angles(pos, half, theta):
    lane = lax.broadcasted_iota(jnp.int32, (1, LANES), 1)
    inv_freq = jnp.exp((lane % half).astype(F32) * (-math.log(theta) / half))
    ang = pos * inv_freq
    return jnp.cos(ang), jnp.sin(ang)


def _proj_kernel(x_ref, g_ref, w_ref, gains_ref, o_ref, h_ref, rope_ref, *, kinds, tm, seq, pos0):
    i = pl.program_id(0)
    j = pl.program_id(1)
    half = ROPE_DIM // 2

    @pl.when(j == 0)
    def _():
        x = x_ref[...]
        ms = jnp.mean(x * x, axis=-1, keepdims=True)
        h_ref[...] = ((x * lax.rsqrt(ms + EPS)) * g_ref[...]).astype(BF16)

    def tiles_of(kind_pred):
        cond = None
        for c, kd in enumerate(kinds):
            if kind_pred(kd):
                t = j == c
                cond = t if cond is None else jnp.logical_or(cond, t)
        return cond

    chunk = _row_tile(tm, EPILOGUE_ROWS)
    chunks = [slice(r, r + chunk) for r in range(0, tm, chunk)]

    def matmul(rows=slice(None)):
        return _dot(h_ref[rows, :], w_ref[...])

    cond = tiles_of(lambda kd: kd[0] == ID)
    if cond is not None:
        @pl.when(cond)
        def _():
            o_ref[...] = matmul()

    cond = tiles_of(lambda kd: kd[0] == SILU)
    if cond is not None:
        @pl.when(cond)
        def _():
            for rows in chunks:
                a = matmul(rows)
                o_ref[rows, :] = a * _sigmoid(a)

    cond = tiles_of(lambda kd: kd[0] == SIGMOID)
    if cond is not None:
        @pl.when(cond)
        def _():
            for rows in chunks:
                o_ref[rows, :] = _sigmoid(matmul(rows))

    rope_tiles = [c for c, kd in enumerate(kinds) if kd[0] == HNORM_ROPE]
    if rope_tiles:
        assert seq % tm == 0 or tm % seq == 0
        local_period = tm if seq % tm == 0 else seq

        @pl.when(jnp.logical_and(i == 0, j == rope_tiles[0]))
        def _():
            r = lax.broadcasted_iota(jnp.int32, (tm, 1), 0)
            cos, sin = _rope_angles(lax.rem(r, local_period).astype(F32), half, ROPE_THETA)
            rope_ref[0] = cos
            rope_ref[1] = sin

    for c, kd in enumerate(kinds):
        if kd[0] not in (HNORM_ROPE, HNORM):
            continue

        @pl.when(j == c)
        def _(kd=kd):
            hd = kd[1]
            gain = gains_ref[kd[2]:kd[2] + 1, :hd]
            if kd[0] == HNORM_ROPE:
                base = pos0 + (lax.rem(i * tm, seq) if seq % tm == 0 else 0 * i)
                cos_b, sin_b = _rope_angles(base.astype(F32), half, ROPE_THETA)
                lane = lax.broadcasted_iota(jnp.int32, (1, LANES), 1)
                in_rope = lane < 2 * half
            for rows in chunks:
                a = matmul(rows)
                if kd[0] == HNORM_ROPE:
                    cos_l, sin_l = rope_ref[0, rows, :], rope_ref[1, rows, :]
                    cos = jnp.where(in_rope, cos_l * cos_b - sin_l * sin_b, 1.0)
                    sin = jnp.where(in_rope, sin_l * cos_b + cos_l * sin_b, 0.0)
                    s_lo = jnp.where(lane < half, -sin, 0.0)
                    s_hi = jnp.where(lane >= half, sin, 0.0)
                for h in range(a.shape[1] // hd):
                    ah = a[:, h * hd:(h + 1) * hd]
                    ms = jnp.mean(ah * ah, axis=-1, keepdims=True)
                    ah = (ah * lax.rsqrt(ms + EPS)) * gain
                    if kd[0] == HNORM_ROPE:
                        ah = (ah * cos + pltpu.roll(ah, LANES - half, 1) * s_lo
                              + pltpu.roll(ah, half, 1) * s_hi)
                    o_ref[rows, h * hd:(h + 1) * hd] = ah


def _normed_proj(x, norm_g, w_bf16, gains, kinds, *, tm, tn, seq, pos0):
    r, d = x.shape
    n = w_bf16.shape[1]
    assert r % tm == 0 and n % tn == 0 and len(kinds) == n // tn
    return pl.pallas_call(
        functools.partial(_proj_kernel, kinds=kinds, tm=tm, seq=seq, pos0=pos0),
        grid=(r // tm, n // tn),
        in_specs=[
            pl.BlockSpec((tm, d), lambda i, j: (i, 0)),
            pl.BlockSpec((1, d), lambda i, j: (0, 0)),
            pl.BlockSpec((d, tn), lambda i, j: (0, j)),
            pl.BlockSpec(gains.shape, lambda i, j: (0, 0)),
        ],
        out_specs=_seg_spec((tm, tn), lambda i, j: (j, i, 0)),
        out_shape=jax.ShapeDtypeStruct((n // tn, r, tn), F32),
        scratch_shapes=[pltpu.VMEM((tm, d), BF16), pltpu.VMEM((2, tm, LANES), F32)],
        compiler_params=pltpu.CompilerParams(
            dimension_semantics=("arbitrary", "arbitrary"), vmem_limit_bytes=VMEM_LIMIT),
        name="normed_proj",
    )(x, norm_g.reshape(1, d), w_bf16, gains)


def _sb_prompt_kernel(q_ref, k_ref, v_ref, g_ref, o_ref, *, tq, scale):
    seq = q_ref.shape[0]
    row = lax.broadcasted_iota(jnp.int32, (tq, tq), 0)
    col = lax.broadcasted_iota(jnp.int32, (tq, tq), 1)
    later = (row > col).astype(BF16)
    strict = col < row

    def log_terms(qb, j):
        start = pl.multiple_of(j * tq, tq)
        return _log_sigmoids(_nt_dot(qb, k_ref[pl.ds(start, tq), :].astype(BF16)) * scale)

    def weights(ls_pos, l_neg, c):
        l_hi, l_lo = _split_bf16(l_neg, 2)
        return jnp.exp(ls_pos + _dot(l_hi, later) + _dot(l_lo, later) + c)

    def values(a, j):
        start = pl.multiple_of(j * tq, tq)
        return _dot(a.astype(BF16), v_ref[pl.ds(start, tq), :].astype(BF16))

    def newest_tiles(qi):
        qb = q_ref[pl.ds(pl.multiple_of(qi * tq, tq), tq), :].astype(BF16)
        ls_pos, ls_neg = log_terms(qb, qi)
        l_neg = jnp.where(strict, ls_neg, 0.0)
        acc = values(jnp.where(strict, weights(ls_pos, l_neg, 0.0), 0.0), qi)
        c = jnp.sum(l_neg, axis=1, keepdims=True)
        has_prev = (jnp.zeros_like(row) + qi) > 0
        jp = jnp.maximum(qi - 1, 0)
        ls_pos, ls_neg = log_terms(qb, jp)
        l_neg = jnp.where(has_prev, ls_neg, 0.0)
        acc = acc + values(jnp.where(has_prev, weights(ls_pos, l_neg, c), 0.0), jp)
        return qb, c + jnp.sum(l_neg, axis=1, keepdims=True), acc

    def older_tiles(qi, qb, c, acc):
        def more(state):
            j, c_max, _, _ = state
            return jnp.logical_and(j >= 0, c_max > SB_UNDERFLOW)

        def tile(state):
            j, _, c, acc = state
            ls_pos, l_neg = log_terms(qb, j)
            acc = acc + values(weights(ls_pos, l_neg, c), j)
            c = c + jnp.sum(l_neg, axis=1, keepdims=True)
            return j - 1, jnp.max(c), c, acc

        return lax.while_loop(more, tile, (qi - 2, jnp.max(c), c, acc))[3]

    n_tiles = seq // tq
    group = max(g for g in range(1, SB_Q_GROUP + 1) if n_tiles % g == 0)

    def q_group(gi, carry):
        tiles = [gi * group + u for u in range(group)]
        newest = [newest_tiles(qi) for qi in tiles]
        for qi, (qb, c, acc) in zip(tiles, newest):
            acc = older_tiles(qi, qb, c, acc)
            rows = pl.ds(pl.multiple_of(qi * tq, tq), tq)
            o_ref[rows, :] = (acc * g_ref[rows, :]).astype(o_ref.dtype)
        return carry

    lax.fori_loop(0, n_tiles // group, q_group, 0)


def _head_specs(seq, d, segs):
    return [_seg_spec((seq, d), functools.partial(lambda b, h, s: (s, b, h), s=s)) for s in segs]


def _sb_prompt(proj, *, batch, seq, heads, segs, tq):
    d = HEAD_DIM
    return pl.pallas_call(
        functools.partial(_sb_prompt_kernel, tq=tq, scale=d ** -0.5),
        grid=(batch, heads),
        in_specs=_head_specs(seq, d, segs),
        out_specs=pl.BlockSpec((seq, d), lambda b, h: (b, h)),
        out_shape=jax.ShapeDtypeStruct((batch * seq, heads * d), BF16),
        compiler_params=pltpu.CompilerParams(
            dimension_semantics=("arbitrary", "arbitrary"), vmem_limit_bytes=VMEM_LIMIT),
        name="sb_prompt",
    )(proj, proj, proj, proj)


def _topk_select(g, valid, topk):
    nb = len(g)
    sel = []
    for m in range(nb):
        rank = jnp.zeros_like(g[m])
        for o in range(nb):
            if o == m:
                continue
            beats = g[o] >= g[m] if o < m else g[o] > g[m]
            rank = rank + jnp.where(beats, valid[o], 0.0)
        sel.append(jnp.where(rank < topk, valid[m], 0.0))
    return sel


def _moba_prompt_kernel(q_ref, k_ref, v_ref, g_ref, o_ref, *, blk, topk, scale):
    seq = q_ref.shape[0]
    nb = seq // blk
    row = lax.broadcasted_iota(jnp.int32, (blk, blk), 0)
    col = lax.broadcasted_iota(jnp.int32, (blk, blk), 1)
    causal = col <= row
    eye = (row == col).astype(BF16)
    km = jnp.concatenate(
        [jnp.mean(k_ref[n * blk:(n + 1) * blk, :], axis=0, keepdims=True) for n in range(nb)],
        axis=0).astype(BF16)
    v1 = jnp.concatenate([v_ref[...].astype(BF16), jnp.ones(v_ref.shape, BF16)], axis=1)

    for i in range(nb):
        q = q_ref[i * blk:(i + 1) * blk, :].astype(BF16)
        width = (i + 1) * blk
        s = _nt_dot(q, k_ref[0:width, :].astype(BF16)) * scale
        tiles = [s[:, n * blk:(n + 1) * blk] for n in range(i + 1)]
        if i > topk:
            g = _nt_dot(km, q)
            sel_t = jnp.concatenate(
                _topk_select([g[n:n + 1, :] for n in range(i)], [1.0] * i, topk)
                + [jnp.zeros((nb - i, blk), F32)], axis=0)
            sel_c = _nt_dot(eye, sel_t.astype(BF16))
            for n in range(i):
                tiles[n] = jnp.where(sel_c[:, n:n + 1] > 0.5, tiles[n], NEG)
        tiles[i] = jnp.where(causal, tiles[i], NEG)
        m = functools.reduce(jnp.maximum, [t[:, k:k + LANES] for t in tiles
                                           for k in range(0, blk, LANES)])
        m = m.max(axis=1, keepdims=True)
        p = jnp.concatenate([jnp.exp(t - m).astype(BF16) for t in tiles], axis=1)
        pv = _dot(p, v1[0:width, :])
        d = v_ref.shape[1]
        o_ref[i * blk:(i + 1) * blk, :] = (
            (pv[:, :d] / pv[:, d:d + 1]) * g_ref[i * blk:(i + 1) * blk, :]).astype(o_ref.dtype)


def _moba_prompt(proj, *, batch, seq, heads, segs):
    d = HEAD_DIM
    return pl.pallas_call(
        functools.partial(_moba_prompt_kernel, blk=MOBA_BLOCK, topk=MOBA_TOPK, scale=d ** -0.5),
        grid=(batch, heads),
        in_specs=_head_specs(seq, d, segs),
        out_specs=pl.BlockSpec((seq, d), lambda b, h: (b, h)),
        out_shape=jax.ShapeDtypeStruct((batch * seq, heads * d), BF16),
        compiler_params=pltpu.CompilerParams(
            dimension_semantics=("arbitrary", "arbitrary"), vmem_limit_bytes=VMEM_LIMIT),
        name="moba_prompt",
    )(proj, proj, proj, proj)


def _mem_prompt_kernel(q_ref, k_ref, v_ref, g_ref, o_ref, *, heads, scale):
    d = q_ref.shape[1] // heads
    for h in range(heads):
        cols = slice(h * d, (h + 1) * d)
        s = _nt_dot(q_ref[:, cols].astype(BF16), k_ref[:, cols].astype(BF16)) * scale
        m = jnp.max(s, axis=1, keepdims=True)
        p = jnp.exp(s - m)
        l = jnp.sum(p, axis=1, keepdims=True)
        o = _dot(p.astype(BF16), v_ref[:, cols].astype(BF16)) / l
        o_ref[:, cols] = (o * g_ref[:, cols]).astype(o_ref.dtype)


def _mem_prompt(proj, memkv, *, batch, seq, n_mem, heads, q_seg, g_seg, tq):
    w = heads * MEM_HEAD_DIM
    nq = seq // tq
    return pl.pallas_call(
        functools.partial(_mem_prompt_kernel, heads=heads, scale=MEM_HEAD_DIM ** -0.5),
        grid=(batch, nq),
        in_specs=[
            _seg_spec((tq, w), lambda b, i: (q_seg, b * nq + i, 0)),
            _seg_spec((n_mem, w), lambda b, i: (0, b, 0)),
            _seg_spec((n_mem, w), lambda b, i: (1, b, 0)),
            _seg_spec((tq, w), lambda b, i: (g_seg, b * nq + i, 0)),
        ],
        out_specs=pl.BlockSpec((tq, w), lambda b, i: (b * nq + i, 0)),
        out_shape=jax.ShapeDtypeStruct((batch * seq, w), BF16),
        compiler_params=pltpu.CompilerParams(
            dimension_semantics=("arbitrary", "arbitrary"), vmem_limit_bytes=VMEM_LIMIT),
        name="mem_prompt",
    )(proj, memkv, memkv, proj)


def _diag_mask(nh, width):
    sub = lax.broadcasted_iota(jnp.int32, (nh, width), 0)
    lane = lax.broadcasted_iota(jnp.int32, (nh, width), 1)
    return (lane % nh) == sub


def _class_scan(x, nh, cyclic):
    width = x.shape[-1]
    lane = lax.broadcasted_iota(jnp.int32, x.shape, x.ndim - 1)
    shift = nh
    while shift < width:
        nxt = pltpu.roll(x, width - shift, x.ndim - 1)
        x = x + (nxt if cyclic else jnp.where(lane + shift < width, nxt, 0.0))
        shift *= 2
    return x


def _sb_pages(q, k_refs, v_refs, z_ref, first_page, c_after, acc, *, scale, q_pos):
    nh = q.shape[0]
    n = len(k_refs)
    width = k_refs[0].shape[1]
    slots = width // nh
    diag = _diag_mask(nh, width)
    for p in range(n):
        z_ref[p:p + 1, :] = jnp.sum(
            jnp.where(diag, _nt_dot(q, k_refs[p][0].astype(BF16)), 0.0), axis=0, keepdims=True)
    z = z_ref[0:n, :] * scale
    prow = lax.broadcasted_iota(jnp.int32, (n, width), 0)
    plane = lax.broadcasted_iota(jnp.int32, (n, width), 1)
    mask = ((first_page + prow) * slots + plane // nh) < q_pos
    ls_pos, ls_neg = _log_sigmoids(z)
    l_neg = jnp.where(mask, ls_neg, 0.0)
    within = _class_scan(l_neg, nh, cyclic=False) - l_neg
    totals = _class_scan(l_neg, nh, cyclic=True)
    pr = lax.broadcasted_iota(jnp.int32, (n, n), 0)
    pc = lax.broadcasted_iota(jnp.int32, (n, n), 1)
    later = (pc > pr).astype(BF16)
    carry = sum(_dot(later, part) for part in _split_bf16(totals, 3))
    a = jnp.where(mask, jnp.exp(ls_pos + within + carry + c_after), 0.0)
    for p in range(n):
        w_p = jnp.where(diag, jnp.broadcast_to(a[p:p + 1, :], (nh, width)), 0.0).astype(BF16)
        acc = acc + _dot(w_p, v_refs[p][0].astype(BF16))
    return acc, c_after + jnp.sum(totals, axis=0, keepdims=True)


def _sb_sample_head_kernel(pt_ref, q_ref, kn_ref, vn_ref, g_ref, *refs, n_head, first_page, scale,
                           past_len):
    del pt_ref
    k_refs, v_refs = refs[:n_head], refs[n_head:2 * n_head]
    o_ref, acc_ref, c_ref, more_ref, z_ref = refs[2 * n_head:]
    nh, d = q_ref.shape[1:]
    width = k_refs[0].shape[1]
    q = q_ref[0].astype(BF16)
    q_pos = past_len
    diag = _diag_mask(nh, LANES)
    lane = lax.broadcasted_iota(jnp.int32, (1, width), 1)

    kn = jnp.concatenate([kn_ref[0], jnp.zeros((LANES - nh, d), F32)], axis=0).astype(BF16)
    vn = jnp.concatenate([vn_ref[0], jnp.zeros((LANES - nh, d), F32)], axis=0).astype(BF16)
    z_n = jnp.sum(jnp.where(diag, _nt_dot(q, kn), 0.0), axis=0, keepdims=True)
    z_n = jnp.concatenate([z_n, jnp.zeros((1, width - LANES), F32)], axis=1) * scale
    mask_n = (past_len + lane // nh) < q_pos
    lsp_n, lsn_n = _log_sigmoids(z_n)
    a_n = jnp.where(mask_n, jnp.exp(lsp_n), 0.0)
    c_new = _class_scan(jnp.where(mask_n, lsn_n, 0.0), nh, cyclic=True)
    w_n = jnp.where(diag, jnp.broadcast_to(a_n[:, :LANES], (nh, LANES)), 0.0).astype(BF16)

    acc, c = _sb_pages(q, k_refs, v_refs, z_ref, first_page, c_new, _dot(w_n, vn),
                       scale=scale, q_pos=q_pos)
    o_ref[0] = acc * g_ref[0]
    acc_ref[0] = acc
    c_ref[0] = c
    more_ref[0] = jnp.where(jnp.max(c) > SB_UNDERFLOW, 1, 0) + jnp.zeros((1, LANES), jnp.int32)


def _sb_sample_tail_kernel(pt_ref, more_ref, src_ref, q_ref, g_ref, acc_ref, c_ref, *refs,
                           n_tail, scale, past_len):
    del pt_ref, src_ref
    k_refs, v_refs = refs[:n_tail], refs[n_tail:2 * n_tail]
    o_ref, z_ref = refs[2 * n_tail:]
    b = pl.program_id(0)

    @pl.when(more_ref[b] == 0)
    def _():
        o_ref[0] = acc_ref[0] * g_ref[0]

    @pl.when(more_ref[b] != 0)
    def _():
        acc, _ = _sb_pages(q_ref[0].astype(BF16), k_refs, v_refs, z_ref, 0, c_ref[0], acc_ref[0],
                           scale=scale, q_pos=past_len)
        o_ref[0] = acc * g_ref[0]


def _sb_sample_finish(page_table, rows4, k_pool, v_pool, head_out, *, n_head):
    q, _, _, gate = rows4
    out, acc, c, more = head_out
    b, nh, d = q.shape
    n_pages = page_table.shape[1]
    rows = k_pool.shape[1]
    n_tail = n_pages - n_head
    more = more[:, 0, 0]

    def tail():
        src = lax.cummax(jnp.where(more > 0, jnp.arange(b, dtype=jnp.int32), 0), axis=0)
        row3 = pl.BlockSpec((1, nh, d), lambda i, pt, mo, sr: (i, 0, 0))

        def tail_page(i, pt, mo, sr, p):
            seq_row = jnp.clip(sr[jnp.minimum(i, b - 1)], 0, b - 1)
            return (pt[seq_row, p], 0, 0)

        tail_pages = [pl.BlockSpec((1, rows, d), functools.partial(tail_page, p=p))
                      for p in range(n_tail)]
        return pl.pallas_call(
            functools.partial(_sb_sample_tail_kernel, n_tail=n_tail, scale=d ** -0.5,
                              past_len=n_pages * (rows // nh)),
            grid_spec=pltpu.PrefetchScalarGridSpec(
                num_scalar_prefetch=3,
                grid=(b,),
                in_specs=[row3, row3, row3,
                          pl.BlockSpec((1, 1, rows), lambda i, pt, mo, sr: (i, 0, 0))]
                + tail_pages + tail_pages,
                out_specs=row3,
                scratch_shapes=[pltpu.VMEM((n_tail, rows), F32)],
            ),
            out_shape=jax.ShapeDtypeStruct((b, nh, d), F32),
            compiler_params=pltpu.CompilerParams(
                dimension_semantics=("arbitrary",), vmem_limit_bytes=VMEM_LIMIT),
            name="sb_sample_tail",
        )(page_table, more, src, q, gate, acc, c, *([k_pool] * n_tail), *([v_pool] * n_tail))

    return lax.cond(jnp.max(more) > 0, tail, lambda: out)


def _moba_sample_kernel(pt_ref, q_ref, kn_ref, vn_ref, g_ref, *refs, n_pages, ppb, topk, scale):
    del pt_ref
    k_refs, v_refs = refs[:n_pages], refs[n_pages:2 * n_pages]
    o_ref, s_ref = refs[2 * n_pages:]
    nh, d = q_ref.shape[1:]
    width = k_refs[0].shape[1]
    slots = width // nh
    nb = n_pages // ppb
    q = q_ref[0].astype(BF16)
    qf = q.astype(F32)
    diag = _diag_mask(nh, width)

    km = []
    for p in range(n_pages):
        kp = k_refs[p][0]
        s_ref[p] = _nt_dot(q, kp.astype(BF16)) * scale
        ksum = jnp.sum(kp.reshape(slots, nh, d), axis=0)
        if p % ppb == 0:
            km.append(ksum)
        else:
            km[-1] = km[-1] + ksum
    g = [jnp.sum(qf * (kb * (1.0 / (ppb * slots))).astype(BF16).astype(F32),
                 axis=1, keepdims=True) for kb in km]
    sel = _topk_select(g, [1.0] * nb, topk)

    s_new = jnp.sum(qf * kn_ref[0].astype(BF16).astype(F32), axis=1, keepdims=True) * scale
    m = s_new
    for p in range(n_pages):
        sp = jnp.where(jnp.logical_and(diag, sel[p // ppb] > 0.5), s_ref[p], NEG)
        s_ref[p] = sp
        m = jnp.maximum(m, jnp.max(sp, axis=1, keepdims=True))
    w_new = jnp.exp(s_new - m)
    l = w_new
    acc = w_new * vn_ref[0]
    for p in range(n_pages):
        pp = jnp.exp(s_ref[p] - m)
        l = l + jnp.sum(pp, axis=1, keepdims=True)
        acc = acc + _dot(pp.astype(BF16), v_refs[p][0].astype(BF16))
    o_ref[0] = (acc / l) * g_ref[0]


def _mem_sample_kernel(q_ref, k_ref, v_ref, g_ref, o_ref, *, scale):
    s = jnp.sum(k_ref[0] * q_ref[...], axis=-1, keepdims=True) * scale
    m = jnp.max(s, axis=0, keepdims=True)
    p = jnp.exp(s - m)
    l = jnp.sum(p, axis=0)
    o_ref[0] = (jnp.sum(p * v_ref[0], axis=0) / l) * g_ref[0]


def _sample_kernel(pt_ref, *refs, n_head, n_pages, first_page, ppb, topk, past_len):
    it = iter(refs)
    take = lambda n: [next(it) for _ in range(n)]
    sb_rows, mb_rows = take(4), take(4)
    mem_q, mem_g, mem_k, mem_v = take(4)
    sb_k, sb_v, mb_k, mb_v = take(n_head), take(n_head), take(n_pages), take(n_pages)
    sb_out, (mb_o, mem_o), (z_ref, s_ref) = take(4), take(2), take(2)
    _moba_sample_kernel(pt_ref, *mb_rows, *mb_k, *mb_v, mb_o, s_ref, n_pages=n_pages, ppb=ppb,
                        topk=topk, scale=mb_rows[0].shape[2] ** -0.5)
    _sb_sample_head_kernel(pt_ref, *sb_rows, *sb_k, *sb_v, *sb_out, z_ref, n_head=n_head,
                           first_page=first_page, scale=sb_rows[0].shape[2] ** -0.5,
                           past_len=past_len)
    _mem_sample_kernel(mem_q, mem_k, mem_v, mem_g, mem_o, scale=mem_q.shape[2] ** -0.5)


def _sample_attention(page_table, sb_rows, mb_rows, mem_rows, sb_pools, mb_pools, mem_caches, *,
                      n_head):
    b, nh, d = sb_rows[0].shape
    mem_nh, mem_d = mem_rows[0].shape[1:]
    n_pages = page_table.shape[1]
    rows = sb_pools[0].shape[1]
    slots = rows // nh
    n_mem = mem_caches[0].shape[1]
    ppb = MOBA_BLOCK // slots
    n_tail = n_pages - n_head
    assert MOBA_BLOCK % slots == 0 and n_pages % ppb == 0 and mb_rows[0].shape == (b, nh, d)

    row = pl.BlockSpec((1, nh, d), lambda i, pt: (i, 0, 0))
    mem_row = pl.BlockSpec((1, mem_nh, mem_d), lambda i, pt: (i, 0, 0))
    cache = pl.BlockSpec((1, n_mem, mem_nh, mem_d), lambda i, pt: (i, 0, 0, 0))
    page = lambda p: pl.BlockSpec((1, rows, d), lambda i, pt: (pt[i, p], 0, 0))
    head_pages = [page(n_tail + p) for p in range(n_head)]
    all_pages = [page(p) for p in range(n_pages)]
    outs = pl.pallas_call(
        functools.partial(_sample_kernel, n_head=n_head, n_pages=n_pages, first_page=n_tail,
                          ppb=ppb, topk=MOBA_TOPK, past_len=n_pages * slots),
        grid_spec=pltpu.PrefetchScalarGridSpec(
            num_scalar_prefetch=1,
            grid=(b,),
            in_specs=[row] * 8 + [mem_row, mem_row, cache, cache]
            + head_pages + head_pages + all_pages + all_pages,
            out_specs=[row, row, pl.BlockSpec((1, 1, rows), lambda i, pt: (i, 0, 0)),
                       pl.BlockSpec((1, 1, LANES), lambda i, pt: (i, 0, 0)), row, mem_row],
            scratch_shapes=[pltpu.VMEM((n_head, rows), F32), pltpu.VMEM((n_pages, nh, rows), F32)],
        ),
        out_shape=[jax.ShapeDtypeStruct((b, nh, d), F32), jax.ShapeDtypeStruct((b, nh, d), F32),
                   jax.ShapeDtypeStruct((b, 1, rows), F32),
                   jax.ShapeDtypeStruct((b, 1, LANES), jnp.int32),
                   jax.ShapeDtypeStruct((b, nh, d), F32),
                   jax.ShapeDtypeStruct((b, mem_nh, mem_d), F32)],
        compiler_params=pltpu.CompilerParams(
            dimension_semantics=("arbitrary",), vmem_limit_bytes=VMEM_LIMIT),
        name="sample_attention",
    )(page_table, *sb_rows, *mb_rows, *mem_rows, *mem_caches,
      *([sb_pools[0]] * n_head), *([sb_pools[1]] * n_head),
      *([mb_pools[0]] * n_pages), *([mb_pools[1]] * n_pages))
    return outs[:4], outs[4], outs[5]


def _merge_out_kernel(ua_ref, ub_ref, um_ref, *refs, n_halves):
    s_refs = refs[:3 * n_halves]
    x_ref, wa_ref, wb_ref, wm_ref, wo_ref, o_ref = refs[3 * n_halves:]
    tn = s_refs[0].shape[1]
    merged = []
    for n in range(n_halves):
        cols = slice(n * tn, (n + 1) * tn)
        part = s_refs[n][...] * _dot(ua_ref[...], wa_ref[:, cols])
        part = part + s_refs[n_halves + n][...] * _dot(ub_ref[...], wb_ref[:, cols])
        part = part + s_refs[2 * n_halves + n][...] * _dot(um_ref[...], wm_ref[:, cols])
        merged.append(part.astype(BF16))
    o_ref[...] = x_ref[...] + _dot(jnp.concatenate(merged, axis=1), wo_ref[...])


def _merge_out(ua, ub, um, proj, x, wa, wb, wm, wo, *, merge_seg0, tm):
    r, e = ua.shape
    d = wa.shape[1]
    tn = proj.shape[2]
    n_halves = d // tn
    assert r % tm == 0 and d % tn == 0
    act = pl.BlockSpec((tm, e), lambda i: (i, 0))
    merges = [_seg_spec((tm, tn), functools.partial(lambda i, s: (s, i, 0), s=merge_seg0 + k))
              for k in range(3 * n_halves)]
    resident = lambda shape: pl.BlockSpec(shape, lambda i: (0, 0), pipeline_mode=pl.Buffered(1))
    return pl.pallas_call(
        functools.partial(_merge_out_kernel, n_halves=n_halves),
        grid=(r // tm,),
        in_specs=[act, act, act] + merges + [
            pl.BlockSpec((tm, d), lambda i: (i, 0)),
            resident((e, d)), resident((e, d)), resident((e, d)), resident((d, d))],
        out_specs=pl.BlockSpec((tm, d), lambda i: (i, 0)),
        out_shape=jax.ShapeDtypeStruct((r, d), F32),
        compiler_params=pltpu.CompilerParams(
            dimension_semantics=("arbitrary",), vmem_limit_bytes=VMEM_LIMIT),
        name="merge_out",
    )(ua, ub, um, *([proj] * (3 * n_halves)), x, wa, wb, wm, wo)


def kernel(x_prompt, x_sample, mem_prompt, cache_sb_k, cache_sb_v, cache_moba_k, cache_moba_v,
           cache_mem_k, cache_mem_v, page_table, norm_g, w_in, qn_moba, kn_moba, qn_mem, kn_mem,
           mem_norm_g, w_mem_kv, w_br_sb, w_br_moba, w_br_mem, w_out):
    batch, seq, d_model = x_prompt.shape
    dec_batch, dec_seq, _ = x_sample.shape
    n_mem = mem_prompt.shape[1]
    n_pool, page = cache_sb_k.shape[:2]
    past_len = page_table.shape[1] * page
    sb_w, mb_w, mem_w = SB_HEADS * HEAD_DIM, MB_HEADS * HEAD_DIM, MEM_HEADS * MEM_HEAD_DIM
    assert sb_w == mb_w == mem_w == 1024 and d_model % 1024 == 0
    assert dec_seq == 1 and past_len % MOBA_BLOCK == 0 and seq % MOBA_BLOCK == 0
    tn = 1024

    names = ["qa", "ka", "va", "ga", "qb", "kb", "vb", "gb", "qm", "gm"]
    sg = {nm: i for i, nm in enumerate(names)}
    merge_seg0 = len(names)
    in_kinds = ((ID,), (ID,), (ID,), (SILU,),
                (HNORM_ROPE, HEAD_DIM, 0), (HNORM_ROPE, HEAD_DIM, 1), (ID,), (SILU,),
                (HNORM, MEM_HEAD_DIM, 2), (SILU,)) + ((SIGMOID,),) * (3 * d_model // tn)
    mem_kinds = ((HNORM, MEM_HEAD_DIM, 3), (ID,))

    pad = lambda g: jnp.pad(g, (0, MEM_HEAD_DIM - g.shape[0]))
    gains = jnp.stack([pad(qn_moba), pad(kn_moba), qn_mem, kn_mem])
    w_in_b, w_mem_b = w_in.astype(BF16), w_mem_kv.astype(BF16)
    wa, wb, wm, wo = (w.astype(BF16) for w in (w_br_sb, w_br_moba, w_br_mem, w_out))

    n_p = batch * seq
    xp = x_prompt.reshape(n_p, d_model)
    memkv = _normed_proj(mem_prompt.reshape(batch * n_mem, d_model), mem_norm_g, w_mem_b, gains,
                         mem_kinds, tm=n_mem, tn=tn, seq=n_mem, pos0=0)
    proj_p = _normed_proj(xp, norm_g, w_in_b, gains, in_kinds, tm=_row_tile(n_p, 1024), tn=tn,
                          seq=seq, pos0=0)
    ua = _sb_prompt(proj_p, batch=batch, seq=seq, heads=SB_HEADS,
                    segs=(sg["qa"], sg["ka"], sg["va"], sg["ga"]), tq=256)
    ub = _moba_prompt(proj_p, batch=batch, seq=seq, heads=MB_HEADS,
                      segs=(sg["qb"], sg["kb"], sg["vb"], sg["gb"]))
    um = _mem_prompt(proj_p, memkv, batch=batch, seq=seq, n_mem=n_mem, heads=MEM_HEADS,
                     q_seg=sg["qm"], g_seg=sg["gm"], tq=512)
    y_p = _merge_out(ua, ub, um, proj_p, xp, wa, wb, wm, wo, merge_seg0=merge_seg0,
                     tm=_row_tile(n_p, 256))

    n_s = dec_batch * dec_seq
    xs = x_sample.reshape(n_s, d_model)
    proj_s = _normed_proj(xs, norm_g, w_in_b, gains, in_kinds, tm=n_s, tn=tn, seq=dec_seq,
                          pos0=past_len)
    hseg = lambda nm, nh: proj_s[sg[nm]].reshape(n_s, nh, tn // nh)
    pool = lambda c: c.reshape(n_pool, page * c.shape[2], c.shape[3])
    rows_a = tuple(hseg(nm, SB_HEADS) for nm in ("qa", "ka", "va", "ga"))
    rows_b = tuple(hseg(nm, MB_HEADS) for nm in ("qb", "kb", "vb", "gb"))
    n_head = min(SB_SAMPLE_HEAD_PAGES, page_table.shape[1] - 1)
    sb_pools = (pool(cache_sb_k), pool(cache_sb_v))
    sb_head, ob_s, om_s = _sample_attention(
        page_table, rows_a, rows_b, (hseg("qm", MEM_HEADS), hseg("gm", MEM_HEADS)), sb_pools,
        (pool(cache_moba_k), pool(cache_moba_v)), (cache_mem_k, cache_mem_v), n_head=n_head)
    oa_s = _sb_sample_finish(page_table, rows_a, *sb_pools, sb_head, n_head=n_head)
    us = [o.reshape(n_s, tn).astype(BF16) for o in (oa_s, ob_s, om_s)]
    y_s = _merge_out(*us, proj_s, xs, wa, wb, wm, wo, merge_seg0=merge_seg0, tm=n_s)

    kv = lambda proj, b, t: [proj[sg[nm]].reshape(b, t, SB_HEADS, HEAD_DIM)
                             for nm in ("ka", "va", "kb", "vb")]
    mk_p = memkv[0].reshape(batch, n_mem, MEM_HEADS, MEM_HEAD_DIM)
    mv_p = memkv[1].reshape(batch, n_mem, MEM_HEADS, MEM_HEAD_DIM)
    return (y_p.reshape(batch, seq, d_model), y_s.reshape(dec_batch, dec_seq, d_model),
            *kv(proj_p, batch, seq), mk_p, mv_p, *kv(proj_s, dec_batch, dec_seq))
```

```python
import functools
import math

import jax
import jax.numpy as jnp
from jax import lax
from jax.experimental import pallas as pl
from jax.experimental.pallas import tpu as pltpu

F32 = jnp.float32
BF16 = jnp.bfloat16

HEAD_DIM = 128
SB_HEADS = 8
MB_HEADS = 8
MEM_HEADS = 4
MEM_HEAD_DIM = 256
MOBA_BLOCK = 256
MOBA_TOPK = 3
ROPE_THETA = 500000.0
ROPE_DIM = HEAD_DIM // 4
EPS = 1e-6

LANES = 128
SUBLANES = 8
NEG = -1e30
SB_UNDERFLOW = -105.0
SB_SAMPLE_HEAD_PAGES = 2
SB_Q_GROUP = 8
EPILOGUE_ROWS = 256
VMEM_LIMIT = 56 * 1024 * 1024

ID, SILU, SIGMOID, HNORM_ROPE, HNORM = range(5)


def _row_tile(rows, target):
    for tile in range(min(rows, target), SUBLANES - 1, -1):
        if rows % tile == 0 and tile % SUBLANES == 0:
            return tile
    return rows


def _nt_dot(a, b):
    return lax.dot_general(a, b, (((1,), (1,)), ((), ())), preferred_element_type=F32)


def _dot(a, b):
    return jnp.dot(a, b, preferred_element_type=F32)


def _sigmoid(x):
    return 0.5 * jnp.tanh(0.5 * x) + 0.5


def _split_bf16(x, parts):
    out = []
    for _ in range(parts - 1):
        hi = x.astype(BF16)
        out.append(hi)
        x = x - hi.astype(F32)
    out.append(x.astype(BF16))
    return out


def _log_sigmoids(z):
    sp = jnp.log1p(jnp.exp(-jnp.abs(z)))
    return jnp.minimum(z, 0.0) - sp, jnp.minimum(-z, 0.0) - sp


def _seg_spec(block, index):
    return pl.BlockSpec((None,) + block, index)


def _rope_angles(pos, half, theta):
    lane = lax.broadcasted_iota(jnp.int32, (1, LANES), 1)
    inv_freq = jnp.exp((lane % half).astype(F32) * (-math.log(theta) / half))
    ang = pos * inv_freq
    return jnp.cos(ang), jnp.sin(ang)


def _proj_kernel(x_ref, g_ref, w_ref, gains_ref, o_ref, *rest, kinds, tm, seq, pos0):
    i = pl.program_id(0)
    j = pl.program_id(1)
    half = ROPE_DIM // 2
    if len(rest) == 3:
        wb_ref, h_ref, rope_ref = rest
        wb_ref[...] = w_ref[...].astype(BF16)
        w_ref = wb_ref
    else:
        h_ref, rope_ref = rest

    @pl.when(j == 0)
    def _():
        x = x_ref[...]
        ms = jnp.mean(x * x, axis=-1, keepdims=True)
        h_ref[...] = ((x * lax.rsqrt(ms + EPS)) * g_ref[...]).astype(BF16)

    def tiles_of(kind_pred):
        cond = None
        for c, kd in enumerate(kinds):
            if kind_pred(kd):
                t = j == c
                cond = t if cond is None else jnp.logical_or(cond, t)
        return cond

    chunk = _row_tile(tm, EPILOGUE_ROWS)
    chunks = [slice(r, r + chunk) for r in range(0, tm, chunk)]

    def matmul(rows=slice(None)):
        return _dot(h_ref[rows, :], w_ref[...])

    cond = tiles_of(lambda kd: kd[0] == ID)
    if cond is not None:
        @pl.when(cond)
        def _():
            o_ref[...] = matmul()

    cond = tiles_of(lambda kd: kd[0] == SILU)
    if cond is not None:
        @pl.when(cond)
        def _():
            for rows in chunks:
                a = matmul(rows)
                o_ref[rows, :] = a * _sigmoid(a)

    cond = tiles_of(lambda kd: kd[0] == SIGMOID)
    if cond is not None:
        @pl.when(cond)
        def _():
            for rows in chunks:
                o_ref[rows, :] = _sigmoid(matmul(rows))

    rope_tiles = [c for c, kd in enumerate(kinds) if kd[0] == HNORM_ROPE]
    if rope_tiles:
        assert seq % tm == 0 or tm % seq == 0
        local_period = tm if seq % tm == 0 else seq

        @pl.when(jnp.logical_and(i == 0, j == rope_tiles[0]))
        def _():
            r = lax.broadcasted_iota(jnp.int32, (tm, 1), 0)
            cos, sin = _rope_angles(lax.rem(r, local_period).astype(F32), half, ROPE_THETA)
            rope_ref[0] = cos
            rope_ref[1] = sin

    for c, kd in enumerate(kinds):
        if kd[0] not in (HNORM_ROPE, HNORM):
            continue

        @pl.when(j == c)
        def _(kd=kd):
            hd = kd[1]
            gain = gains_ref[kd[2]:kd[2] + 1, :hd]
            if kd[0] == HNORM_ROPE:
                base = pos0 + (lax.rem(i * tm, seq) if seq % tm == 0 else 0 * i)
                cos_b, sin_b = _rope_angles(base.astype(F32), half, ROPE_THETA)
                lane = lax.broadcasted_iota(jnp.int32, (1, LANES), 1)
                in_rope = lane < 2 * half
            for rows in chunks:
                a = matmul(rows)
                if kd[0] == HNORM_ROPE:
                    cos_l, sin_l = rope_ref[0, rows, :], rope_ref[1, rows, :]
                    cos = jnp.where(in_rope, cos_l * cos_b - sin_l * sin_b, 1.0)
                    sin = jnp.where(in_rope, sin_l * cos_b + cos_l * sin_b, 0.0)
                    s_lo = jnp.where(lane < half, -sin, 0.0)
                    s_hi = jnp.where(lane >= half, sin, 0.0)
                for h in range(a.shape[1] // hd):
                    ah = a[:, h * hd:(h + 1) * hd]
                    ms = jnp.mean(ah * ah, axis=-1, keepdims=True)
                    ah = (ah * lax.rsqrt(ms + EPS)) * gain
                    if kd[0] == HNORM_ROPE:
                        ah = (ah * cos + pltpu.roll(ah, LANES - half, 1) * s_lo
                              + pltpu.roll(ah, half, 1) * s_hi)
                    o_ref[rows, h * hd:(h + 1) * hd] = ah


def _normed_proj(x, norm_g, w, gains, kinds, *, tm, tn, seq, pos0):
    r, d = x.shape
    n = w.shape[1]
    assert r % tm == 0 and n % tn == 0 and len(kinds) == n // tn
    w_spec = pl.BlockSpec((d, tn), lambda i, j: (0, j))
    out_specs = _seg_spec((tm, tn), lambda i, j: (j, i, 0))
    out_shape = jax.ShapeDtypeStruct((n // tn, r, tn), F32)
    if w.dtype != BF16:
        assert r == tm
        out_specs, out_shape = [out_specs, w_spec], [out_shape, jax.ShapeDtypeStruct(w.shape, BF16)]
    return pl.pallas_call(
        functools.partial(_proj_kernel, kinds=kinds, tm=tm, seq=seq, pos0=pos0),
        grid=(r // tm, n // tn),
        in_specs=[
            pl.BlockSpec((tm, d), lambda i, j: (i, 0)),
            pl.BlockSpec((1, d), lambda i, j: (0, 0)),
            w_spec,
            pl.BlockSpec(gains.shape, lambda i, j: (0, 0)),
        ],
        out_specs=out_specs,
        out_shape=out_shape,
        scratch_shapes=[pltpu.VMEM((tm, d), BF16), pltpu.VMEM((2, tm, LANES), F32)],
        compiler_params=pltpu.CompilerParams(
            dimension_semantics=("arbitrary", "arbitrary"), vmem_limit_bytes=VMEM_LIMIT),
        name="normed_proj",
    )(x, norm_g.reshape(1, d), w, gains)


def _sb_prompt_kernel(q_ref, k_ref, v_ref, g_ref, o_ref, *, tq, scale):
    seq = q_ref.shape[0]
    row = lax.broadcasted_iota(jnp.int32, (tq, tq), 0)
    col = lax.broadcasted_iota(jnp.int32, (tq, tq), 1)
    later = (row > col).astype(BF16)
    strict = col < row

    def log_terms(qb, j):
        start = pl.multiple_of(j * tq, tq)
        return _log_sigmoids(_nt_dot(qb, k_ref[pl.ds(start, tq), :].astype(BF16)) * scale)

    def weights(ls_pos, l_neg, c):
        l_hi, l_lo = _split_bf16(l_neg, 2)
        return jnp.exp(ls_pos + _dot(l_hi, later) + _dot(l_lo, later) + c)

    def values(a, j):
        start = pl.multiple_of(j * tq, tq)
        return _dot(a.astype(BF16), v_ref[pl.ds(start, tq), :].astype(BF16))

    def newest_tiles(qi):
        qb = q_ref[pl.ds(pl.multiple_of(qi * tq, tq), tq), :].astype(BF16)
        ls_pos, ls_neg = log_terms(qb, qi)
        l_neg = jnp.where(strict, ls_neg, 0.0)
        acc = values(jnp.where(strict, weights(ls_pos, l_neg, 0.0), 0.0), qi)
        c = jnp.sum(l_neg, axis=1, keepdims=True)
        has_prev = (jnp.zeros_like(row) + qi) > 0
        jp = jnp.maximum(qi - 1, 0)
        ls_pos, ls_neg = log_terms(qb, jp)
        l_neg = jnp.where(has_prev, ls_neg, 0.0)
        acc = acc + values(jnp.where(has_prev, weights(ls_pos, l_neg, c), 0.0), jp)
        return qb, c + jnp.sum(l_neg, axis=1, keepdims=True), acc

    def older_tiles(qi, qb, c, acc):
        def more(state):
            j, c_max, _, _ = state
            return jnp.logical_and(j >= 0, c_max > SB_UNDERFLOW)

        def tile(state):
            j, _, c, acc = state
            ls_pos, l_neg = log_terms(qb, j)
            acc = acc + values(weights(ls_pos, l_neg, c), j)
            c = c + jnp.sum(l_neg, axis=1, keepdims=True)
            return j - 1, jnp.max(c), c, acc

        return lax.while_loop(more, tile, (qi - 2, jnp.max(c), c, acc))[3]

    n_tiles = seq // tq
    group = max(g for g in range(1, SB_Q_GROUP + 1) if n_tiles % g == 0)

    def q_group(gi, carry):
        tiles = [gi * group + u for u in range(group)]
        newest = [newest_tiles(qi) for qi in tiles]
        for qi, (qb, c, acc) in zip(tiles, newest):
            acc = older_tiles(qi, qb, c, acc)
            rows = pl.ds(pl.multiple_of(qi * tq, tq), tq)
            o_ref[rows, :] = (acc * g_ref[rows, :]).astype(o_ref.dtype)
        return carry

    lax.fori_loop(0, n_tiles // group, q_group, 0)


def _head_specs(seq, d, segs):
    return [_seg_spec((seq, d), functools.partial(lambda b, h, s: (s, b, h), s=s)) for s in segs]


def _sb_prompt(proj, *, batch, seq, heads, segs, tq):
    d = HEAD_DIM
    return pl.pallas_call(
        functools.partial(_sb_prompt_kernel, tq=tq, scale=d ** -0.5),
        grid=(batch, heads),
        in_specs=_head_specs(seq, d, segs),
        out_specs=pl.BlockSpec((seq, d), lambda b, h: (b, h)),
        out_shape=jax.ShapeDtypeStruct((batch * seq, heads * d), BF16),
        compiler_params=pltpu.CompilerParams(
            dimension_semantics=("arbitrary", "arbitrary"), vmem_limit_bytes=VMEM_LIMIT),
        name="sb_prompt",
    )(proj, proj, proj, proj)


def _topk_select(g, valid, topk):
    nb = len(g)
    sel = []
    for m in range(nb):
        rank = jnp.zeros_like(g[m])
        for o in range(nb):
            if o == m:
                continue
            beats = g[o] >= g[m] if o < m else g[o] > g[m]
            rank = rank + jnp.where(beats, valid[o], 0.0)
        sel.append(jnp.where(rank < topk, valid[m], 0.0))
    return sel


def _moba_prompt_kernel(q_ref, k_ref, v_ref, g_ref, o_ref, *, blk, topk, scale):
    seq = q_ref.shape[0]
    nb = seq // blk
    row = lax.broadcasted_iota(jnp.int32, (blk, blk), 0)
    col = lax.broadcasted_iota(jnp.int32, (blk, blk), 1)
    causal = col <= row
    eye = (row == col).astype(BF16)
    km = jnp.concatenate(
        [jnp.mean(k_ref[n * blk:(n + 1) * blk, :], axis=0, keepdims=True) for n in range(nb)],
        axis=0).astype(BF16)
    v1 = jnp.concatenate([v_ref[...].astype(BF16), jnp.ones(v_ref.shape, BF16)], axis=1)

    for i in range(nb):
        q = q_ref[i * blk:(i + 1) * blk, :].astype(BF16)
        width = (i + 1) * blk
        s = _nt_dot(q, k_ref[0:width, :].astype(BF16)) * scale
        tiles = [s[:, n * blk:(n + 1) * blk] for n in range(i + 1)]
        if i > topk:
            g = _nt_dot(km, q)
            sel_t = jnp.concatenate(
                _topk_select([g[n:n + 1, :] for n in range(i)], [1.0] * i, topk)
                + [jnp.zeros((nb - i, blk), F32)], axis=0)
            sel_c = _nt_dot(eye, sel_t.astype(BF16))
            for n in range(i):
                tiles[n] = jnp.where(sel_c[:, n:n + 1] > 0.5, tiles[n], NEG)
        tiles[i] = jnp.where(causal, tiles[i], NEG)
        m = functools.reduce(jnp.maximum, [t[:, k:k + LANES] for t in tiles
                                           for k in range(0, blk, LANES)])
        m = m.max(axis=1, keepdims=True)
        p = jnp.concatenate([jnp.exp(t - m).astype(BF16) for t in tiles], axis=1)
        pv = _dot(p, v1[0:width, :])
        d = v_ref.shape[1]
        o_ref[i * blk:(i + 1) * blk, :] = (
            (pv[:, :d] / pv[:, d:d + 1]) * g_ref[i * blk:(i + 1) * blk, :]).astype(o_ref.dtype)


def _moba_prompt(proj, *, batch, seq, heads, segs):
    d = HEAD_DIM
    return pl.pallas_call(
        functools.partial(_moba_prompt_kernel, blk=MOBA_BLOCK, topk=MOBA_TOPK, scale=d ** -0.5),
        grid=(batch, heads),
        in_specs=_head_specs(seq, d, segs),
        out_specs=pl.BlockSpec((seq, d), lambda b, h: (b, h)),
        out_shape=jax.ShapeDtypeStruct((batch * seq, heads * d), BF16),
        compiler_params=pltpu.CompilerParams(
            dimension_semantics=("arbitrary", "arbitrary"), vmem_limit_bytes=VMEM_LIMIT),
        name="moba_prompt",
    )(proj, proj, proj, proj)


def _mem_prompt_kernel(q_ref, k_ref, v_ref, g_ref, o_ref, *, heads, scale):
    d = q_ref.shape[1] // heads
    for h in range(heads):
        cols = slice(h * d, (h + 1) * d)
        s = _nt_dot(q_ref[:, cols].astype(BF16), k_ref[:, cols].astype(BF16)) * scale
        m = jnp.max(s, axis=1, keepdims=True)
        p = jnp.exp(s - m)
        l = jnp.sum(p, axis=1, keepdims=True)
        o = _dot(p.astype(BF16), v_ref[:, cols].astype(BF16)) / l
        o_ref[:, cols] = (o * g_ref[:, cols]).astype(o_ref.dtype)


def _mem_prompt(proj, memkv, *, batch, seq, n_mem, heads, q_seg, g_seg, tq):
    w = heads * MEM_HEAD_DIM
    nq = seq // tq
    return pl.pallas_call(
        functools.partial(_mem_prompt_kernel, heads=heads, scale=MEM_HEAD_DIM ** -0.5),
        grid=(batch, nq),
        in_specs=[
            _seg_spec((tq, w), lambda b, i: (q_seg, b * nq + i, 0)),
            _seg_spec((n_mem, w), lambda b, i: (0, b, 0)),
            _seg_spec((n_mem, w), lambda b, i: (1, b, 0)),
            _seg_spec((tq, w), lambda b, i: (g_seg, b * nq + i, 0)),
        ],
        out_specs=pl.BlockSpec((tq, w), lambda b, i: (b * nq + i, 0)),
        out_shape=jax.ShapeDtypeStruct((batch * seq, w), BF16),
        compiler_params=pltpu.CompilerParams(
            dimension_semantics=("arbitrary", "arbitrary"), vmem_limit_bytes=VMEM_LIMIT),
        name="mem_prompt",
    )(proj, memkv, memkv, proj)


def _diag_mask(nh, width):
    sub = lax.broadcasted_iota(jnp.int32, (nh, width), 0)
    lane = lax.broadcasted_iota(jnp.int32, (nh, width), 1)
    return (lane % nh) == sub


def _class_scan(x, nh, cyclic):
    width = x.shape[-1]
    lane = lax.broadcasted_iota(jnp.int32, x.shape, x.ndim - 1)
    shift = nh
    while shift < width:
        nxt = pltpu.roll(x, width - shift, x.ndim - 1)
        x = x + (nxt if cyclic else jnp.where(lane + shift < width, nxt, 0.0))
        shift *= 2
    return x


def _sb_pages(q, k_refs, v_refs, z_ref, first_page, c_after, acc, *, scale, q_pos):
    nh = q.shape[0]
    n = len(k_refs)
    width = k_refs[0].shape[1]
    slots = width // nh
    diag = _diag_mask(nh, width)
    for p in range(n):
        z_ref[p:p + 1, :] = jnp.sum(
            jnp.where(diag, _nt_dot(q, k_refs[p][0].astype(BF16)), 0.0), axis=0, keepdims=True)
    z = z_ref[0:n, :] * scale
    prow = lax.broadcasted_iota(jnp.int32, (n, width), 0)
    plane = lax.broadcasted_iota(jnp.int32, (n, width), 1)
    mask = ((first_page + prow) * slots + plane // nh) < q_pos
    ls_pos, ls_neg = _log_sigmoids(z)
    l_neg = jnp.where(mask, ls_neg, 0.0)
    within = _class_scan(l_neg, nh, cyclic=False) - l_neg
    totals = _class_scan(l_neg, nh, cyclic=True)
    pr = lax.broadcasted_iota(jnp.int32, (n, n), 0)
    pc = lax.broadcasted_iota(jnp.int32, (n, n), 1)
    later = (pc > pr).astype(BF16)
    carry = sum(_dot(later, part) for part in _split_bf16(totals, 3))
    a = jnp.where(mask, jnp.exp(ls_pos + within + carry + c_after), 0.0)
    for p in range(n):
        w_p = jnp.where(diag, jnp.broadcast_to(a[p:p + 1, :], (nh, width)), 0.0).astype(BF16)
        acc = acc + _dot(w_p, v_refs[p][0].astype(BF16))
    return acc, c_after + jnp.sum(totals, axis=0, keepdims=True)


def _sb_sample_head_kernel(pt_ref, q_ref, kn_ref, vn_ref, g_ref, *refs, n_head, first_page, scale,
                           past_len):
    del pt_ref
    k_refs, v_refs = refs[:n_head], refs[n_head:2 * n_head]
    o_ref, acc_ref, c_ref, more_ref, z_ref = refs[2 * n_head:]
    nh, d = q_ref.shape[1:]
    width = k_refs[0].shape[1]
    q = q_ref[0].astype(BF16)
    q_pos = past_len
    diag = _diag_mask(nh, LANES)
    lane = lax.broadcasted_iota(jnp.int32, (1, width), 1)

    kn = jnp.concatenate([kn_ref[0], jnp.zeros((LANES - nh, d), F32)], axis=0).astype(BF16)
    vn = jnp.concatenate([vn_ref[0], jnp.zeros((LANES - nh, d), F32)], axis=0).astype(BF16)
    z_n = jnp.sum(jnp.where(diag, _nt_dot(q, kn), 0.0), axis=0, keepdims=True)
    z_n = jnp.concatenate([z_n, jnp.zeros((1, width - LANES), F32)], axis=1) * scale
    mask_n = (past_len + lane // nh) < q_pos
    lsp_n, lsn_n = _log_sigmoids(z_n)
    a_n = jnp.where(mask_n, jnp.exp(lsp_n), 0.0)
    c_new = _class_scan(jnp.where(mask_n, lsn_n, 0.0), nh, cyclic=True)
    w_n = jnp.where(diag, jnp.broadcast_to(a_n[:, :LANES], (nh, LANES)), 0.0).astype(BF16)

    acc, c = _sb_pages(q, k_refs, v_refs, z_ref, first_page, c_new, _dot(w_n, vn),
                       scale=scale, q_pos=q_pos)
    o_ref[0] = acc * g_ref[0]
    acc_ref[0] = acc
    c_ref[0] = c
    more_ref[0] = jnp.where(jnp.max(c) > SB_UNDERFLOW, 1, 0) + jnp.zeros((1, LANES), jnp.int32)


def _sb_sample_tail_kernel(pt_ref, more_ref, src_ref, q_ref, g_ref, acc_ref, c_ref, *refs,
                           n_tail, scale, past_len):
    del pt_ref, src_ref
    k_refs, v_refs = refs[:n_tail], refs[n_tail:2 * n_tail]
    o_ref, z_ref = refs[2 * n_tail:]
    b = pl.program_id(0)

    @pl.when(more_ref[b] == 0)
    def _():
        o_ref[0] = acc_ref[0] * g_ref[0]

    @pl.when(more_ref[b] != 0)
    def _():
        acc, _ = _sb_pages(q_ref[0].astype(BF16), k_refs, v_refs, z_ref, 0, c_ref[0], acc_ref[0],
                           scale=scale, q_pos=past_len)
        o_ref[0] = acc * g_ref[0]


def _sb_sample_finish(page_table, rows4, k_pool, v_pool, head_out, *, n_head):
    q, _, _, gate = rows4
    out, acc, c, more = head_out
    b, nh, d = q.shape
    n_pages = page_table.shape[1]
    rows = k_pool.shape[1]
    n_tail = n_pages - n_head
    more = more[:, 0, 0]

    def tail():
        src = lax.cummax(jnp.where(more > 0, jnp.arange(b, dtype=jnp.int32), 0), axis=0)
        row3 = pl.BlockSpec((1, nh, d), lambda i, pt, mo, sr: (i, 0, 0))

        def tail_page(i, pt, mo, sr, p):
            seq_row = jnp.clip(sr[jnp.minimum(i, b - 1)], 0, b - 1)
            return (pt[seq_row, p], 0, 0)

        tail_pages = [pl.BlockSpec((1, rows, d), functools.partial(tail_page, p=p))
                      for p in range(n_tail)]
        return pl.pallas_call(
            functools.partial(_sb_sample_tail_kernel, n_tail=n_tail, scale=d ** -0.5,
                              past_len=n_pages * (rows // nh)),
            grid_spec=pltpu.PrefetchScalarGridSpec(
                num_scalar_prefetch=3,
                grid=(b,),
                in_specs=[row3, row3, row3,
                          pl.BlockSpec((1, 1, rows), lambda i, pt, mo, sr: (i, 0, 0))]
                + tail_pages + tail_pages,
                out_specs=row3,
                scratch_shapes=[pltpu.VMEM((n_tail, rows), F32)],
            ),
            out_shape=jax.ShapeDtypeStruct((b, nh, d), F32),
            compiler_params=pltpu.CompilerParams(
                dimension_semantics=("arbitrary",), vmem_limit_bytes=VMEM_LIMIT),
            name="sb_sample_tail",
        )(page_table, more, src, q, gate, acc, c, *([k_pool] * n_tail), *([v_pool] * n_tail))

    return lax.cond(jnp.max(more) > 0, tail, lambda: out)


def _moba_sample_kernel(pt_ref, q_ref, kn_ref, vn_ref, g_ref, *refs, n_pages, ppb, topk, scale):
    del pt_ref
    k_refs, v_refs = refs[:n_pages], refs[n_pages:2 * n_pages]
    o_ref, s_ref = refs[2 * n_pages:]
    nh, d = q_ref.shape[1:]
    width = k_refs[0].shape[1]
    slots = width // nh
    nb = n_pages // ppb
    q = q_ref[0].astype(BF16)
    qf = q.astype(F32)
    diag = _diag_mask(nh, width)

    km = []
    for p in range(n_pages):
        kp = k_refs[p][0]
        s_ref[p] = _nt_dot(q, kp.astype(BF16)) * scale
        ksum = jnp.sum(kp.reshape(slots, nh, d), axis=0)
        if p % ppb == 0:
            km.append(ksum)
        else:
            km[-1] = km[-1] + ksum
    g = [jnp.sum(qf * (kb * (1.0 / (ppb * slots))).astype(BF16).astype(F32),
                 axis=1, keepdims=True) for kb in km]
    sel = _topk_select(g, [1.0] * nb, topk)

    s_new = jnp.sum(qf * kn_ref[0].astype(BF16).astype(F32), axis=1, keepdims=True) * scale
    m = s_new
    for p in range(n_pages):
        sp = jnp.where(jnp.logical_and(diag, sel[p // ppb] > 0.5), s_ref[p], NEG)
        s_ref[p] = sp
        m = jnp.maximum(m, jnp.max(sp, axis=1, keepdims=True))
    w_new = jnp.exp(s_new - m)
    l = w_new
    acc = w_new * vn_ref[0]
    for p in range(n_pages):
        pp = jnp.exp(s_ref[p] - m)
        l = l + jnp.sum(pp, axis=1, keepdims=True)
        acc = acc + _dot(pp.astype(BF16), v_refs[p][0].astype(BF16))
    o_ref[0] = (acc / l) * g_ref[0]


def _mem_sample_kernel(q_ref, k_ref, v_ref, g_ref, o_ref, *, scale):
    s = jnp.sum(k_ref[0] * q_ref[...], axis=-1, keepdims=True) * scale
    m = jnp.max(s, axis=0, keepdims=True)
    p = jnp.exp(s - m)
    l = jnp.sum(p, axis=0)
    o_ref[0] = (jnp.sum(p * v_ref[0], axis=0) / l) * g_ref[0]


def _sample_kernel(pt_ref, *refs, n_head, n_pages, first_page, ppb, topk, past_len):
    it = iter(refs)
    take = lambda n: [next(it) for _ in range(n)]
    sb_rows, mb_rows = take(4), take(4)
    mem_q, mem_g, mem_k, mem_v = take(4)
    sb_k, sb_v, mb_k, mb_v = take(n_head), take(n_head), take(n_pages), take(n_pages)
    sb_out, (mb_o, mem_o), (z_ref, s_ref) = take(4), take(2), take(2)
    _moba_sample_kernel(pt_ref, *mb_rows, *mb_k, *mb_v, mb_o, s_ref, n_pages=n_pages, ppb=ppb,
                        topk=topk, scale=mb_rows[0].shape[2] ** -0.5)
    _sb_sample_head_kernel(pt_ref, *sb_rows, *sb_k, *sb_v, *sb_out, z_ref, n_head=n_head,
                           first_page=first_page, scale=sb_rows[0].shape[2] ** -0.5,
                           past_len=past_len)
    _mem_sample_kernel(mem_q, mem_k, mem_v, mem_g, mem_o, scale=mem_q.shape[2] ** -0.5)


def _sample_attention(page_table, sb_rows, mb_rows, mem_rows, sb_pools, mb_pools, mem_caches, *,
                      n_head):
    b, nh, d = sb_rows[0].shape
    mem_nh, mem_d = mem_rows[0].shape[1:]
    n_pages = page_table.shape[1]
    rows = sb_pools[0].shape[1]
    slots = rows // nh
    n_mem = mem_caches[0].shape[1]
    ppb = MOBA_BLOCK // slots
    n_tail = n_pages - n_head
    assert MOBA_BLOCK % slots == 0 and n_pages % ppb == 0 and mb_rows[0].shape == (b, nh, d)

    row = pl.BlockSpec((1, nh, d), lambda i, pt: (i, 0, 0))
    mem_row = pl.BlockSpec((1, mem_nh, mem_d), lambda i, pt: (i, 0, 0))
    cache = pl.BlockSpec((1, n_mem, mem_nh, mem_d), lambda i, pt: (i, 0, 0, 0))
    page = lambda p: pl.BlockSpec((1, rows, d), lambda i, pt: (pt[i, p], 0, 0))
    head_pages = [page(n_tail + p) for p in range(n_head)]
    all_pages = [page(p) for p in range(n_pages)]
    outs = pl.pallas_call(
        functools.partial(_sample_kernel, n_head=n_head, n_pages=n_pages, first_page=n_tail,
                          ppb=ppb, topk=MOBA_TOPK, past_len=n_pages * slots),
        grid_spec=pltpu.PrefetchScalarGridSpec(
            num_scalar_prefetch=1,
            grid=(b,),
            in_specs=[row] * 8 + [mem_row, mem_row, cache, cache]
            + head_pages + head_pages + all_pages + all_pages,
            out_specs=[row, row, pl.BlockSpec((1, 1, rows), lambda i, pt: (i, 0, 0)),
                       pl.BlockSpec((1, 1, LANES), lambda i, pt: (i, 0, 0)), row, mem_row],
            scratch_shapes=[pltpu.VMEM((n_head, rows), F32), pltpu.VMEM((n_pages, nh, rows), F32)],
        ),
        out_shape=[jax.ShapeDtypeStruct((b, nh, d), F32), jax.ShapeDtypeStruct((b, nh, d), F32),
                   jax.ShapeDtypeStruct((b, 1, rows), F32),
                   jax.ShapeDtypeStruct((b, 1, LANES), jnp.int32),
                   jax.ShapeDtypeStruct((b, nh, d), F32),
                   jax.ShapeDtypeStruct((b, mem_nh, mem_d), F32)],
        compiler_params=pltpu.CompilerParams(
            dimension_semantics=("arbitrary",), vmem_limit_bytes=VMEM_LIMIT),
        name="sample_attention",
    )(page_table, *sb_rows, *mb_rows, *mem_rows, *mem_caches,
      *([sb_pools[0]] * n_head), *([sb_pools[1]] * n_head),
      *([mb_pools[0]] * n_pages), *([mb_pools[1]] * n_pages))
    return outs[:4], outs[4], outs[5]


def _merge_out_kernel(ua_ref, ub_ref, um_ref, *refs, n_halves):
    s_refs = refs[:3 * n_halves]
    x_ref, wa_ref, wb_ref, wm_ref, wo_ref, o_ref = refs[3 * n_halves:]
    tn = s_refs[0].shape[1]
    merged = []
    for n in range(n_halves):
        cols = slice(n * tn, (n + 1) * tn)
        part = s_refs[n][...] * _dot(ua_ref[...], wa_ref[:, cols])
        part = part + s_refs[n_halves + n][...] * _dot(ub_ref[...], wb_ref[:, cols])
        part = part + s_refs[2 * n_halves + n][...] * _dot(um_ref[...], wm_ref[:, cols])
        merged.append(part.astype(BF16))
    o_ref[...] = x_ref[...] + _dot(jnp.concatenate(merged, axis=1), wo_ref[...])


def _merge_out(ua, ub, um, proj, x, wa, wb, wm, wo, *, merge_seg0, tm):
    r, e = ua.shape
    d = wa.shape[1]
    tn = proj.shape[2]
    n_halves = d // tn
    assert r % tm == 0 and d % tn == 0
    act = pl.BlockSpec((tm, e), lambda i: (i, 0))
    merges = [_seg_spec((tm, tn), functools.partial(lambda i, s: (s, i, 0), s=merge_seg0 + k))
              for k in range(3 * n_halves)]
    resident = lambda shape: pl.BlockSpec(shape, lambda i: (0, 0), pipeline_mode=pl.Buffered(1))
    return pl.pallas_call(
        functools.partial(_merge_out_kernel, n_halves=n_halves),
        grid=(r // tm,),
        in_specs=[act, act, act] + merges + [
            pl.BlockSpec((tm, d), lambda i: (i, 0)),
            resident((e, d)), resident((e, d)), resident((e, d)), resident((d, d))],
        out_specs=pl.BlockSpec((tm, d), lambda i: (i, 0)),
        out_shape=jax.ShapeDtypeStruct((r, d), F32),
        compiler_params=pltpu.CompilerParams(
            dimension_semantics=("arbitrary",), vmem_limit_bytes=VMEM_LIMIT),
        name="merge_out",
    )(ua, ub, um, *([proj] * (3 * n_halves)), x, wa, wb, wm, wo)


def kernel(x_prompt, x_sample, mem_prompt, cache_sb_k, cache_sb_v, cache_moba_k, cache_moba_v,
           cache_mem_k, cache_mem_v, page_table, norm_g, w_in, qn_moba, kn_moba, qn_mem, kn_mem,
           mem_norm_g, w_mem_kv, w_br_sb, w_br_moba, w_br_mem, w_out):
    batch, seq, d_model = x_prompt.shape
    dec_batch, dec_seq, _ = x_sample.shape
    n_mem = mem_prompt.shape[1]
    n_pool, page = cache_sb_k.shape[:2]
    past_len = page_table.shape[1] * page
    sb_w, mb_w, mem_w = SB_HEADS * HEAD_DIM, MB_HEADS * HEAD_DIM, MEM_HEADS * MEM_HEAD_DIM
    assert sb_w == mb_w == mem_w == 1024 and d_model % 1024 == 0
    assert dec_seq == 1 and past_len % MOBA_BLOCK == 0 and seq % MOBA_BLOCK == 0
    tn = 1024

    names = ["qa", "ka", "va", "ga", "qb", "kb", "vb", "gb", "qm", "gm"]
    sg = {nm: i for i, nm in enumerate(names)}
    merge_seg0 = len(names)
    in_kinds = ((ID,), (ID,), (ID,), (SILU,),
                (HNORM_ROPE, HEAD_DIM, 0), (HNORM_ROPE, HEAD_DIM, 1), (ID,), (SILU,),
                (HNORM, MEM_HEAD_DIM, 2), (SILU,)) + ((SIGMOID,),) * (3 * d_model // tn)
    mem_kinds = ((HNORM, MEM_HEAD_DIM, 3), (ID,))

    pad = lambda g: jnp.pad(g, (0, MEM_HEAD_DIM - g.shape[0]))
    gains = jnp.stack([pad(qn_moba), pad(kn_moba), qn_mem, kn_mem])
    w_mem_b = w_mem_kv.astype(BF16)
    n_s = dec_batch * dec_seq
    xs = x_sample.reshape(n_s, d_model)
    proj_s, w_in_b = _normed_proj(xs, norm_g, w_in, gains, in_kinds, tm=n_s, tn=tn, seq=dec_seq,
                                  pos0=past_len)
    wa, wb, wm, wo = (w.astype(BF16) for w in (w_br_sb, w_br_moba, w_br_mem, w_out))

    n_p = batch * seq
    xp = x_prompt.reshape(n_p, d_model)
    memkv = _normed_proj(mem_prompt.reshape(batch * n_mem, d_model), mem_norm_g, w_mem_b, gains,
                         mem_kinds, tm=_row_tile(batch * n_mem, 1024), tn=tn, seq=n_mem, pos0=0)
    proj_p = _normed_proj(xp, norm_g, w_in_b, gains, in_kinds, tm=_row_tile(n_p, 1024), tn=tn,
                          seq=seq, pos0=0)
    ua = _sb_prompt(proj_p, batch=batch, seq=seq, heads=SB_HEADS,
                    segs=(sg["qa"], sg["ka"], sg["va"], sg["ga"]), tq=256)
    ub = _moba_prompt(proj_p, batch=batch, seq=seq, heads=MB_HEADS,
                      segs=(sg["qb"], sg["kb"], sg["vb"], sg["gb"]))
    um = _mem_prompt(proj_p, memkv, batch=batch, seq=seq, n_mem=n_mem, heads=MEM_HEADS,
                     q_seg=sg["qm"], g_seg=sg["gm"], tq=512)
    y_p = _merge_out(ua, ub, um, proj_p, xp, wa, wb, wm, wo, merge_seg0=merge_seg0,
                     tm=_row_tile(n_p, 256))

    hseg = lambda nm, nh: proj_s[sg[nm]].reshape(n_s, nh, tn // nh)
    pool = lambda c: c.reshape(n_pool, page * c.shape[2], c.shape[3])
    rows_a = tuple(hseg(nm, SB_HEADS) for nm in ("qa", "ka", "va", "ga"))
    rows_b = tuple(hseg(nm, MB_HEADS) for nm in ("qb", "kb", "vb", "gb"))
    n_head = min(SB_SAMPLE_HEAD_PAGES, page_table.shape[1] - 1)
    sb_pools = (pool(cache_sb_k), pool(cache_sb_v))
    sb_head, ob_s, om_s = _sample_attention(
        page_table, rows_a, rows_b, (hseg("qm", MEM_HEADS), hseg("gm", MEM_HEADS)), sb_pools,
        (pool(cache_moba_k), pool(cache_moba_v)), (cache_mem_k, cache_mem_v), n_head=n_head)
    oa_s = _sb_sample_finish(page_table, rows_a, *sb_pools, sb_head, n_head=n_head)
    us = [o.reshape(n_s, tn).astype(BF16) for o in (oa_s, ob_s, om_s)]
    y_s = _merge_out(*us, proj_s, xs, wa, wb, wm, wo, merge_seg0=merge_seg0, tm=n_s)

    kv = lambda proj, b, t: [proj[sg[nm]].reshape(b, t, SB_HEADS, HEAD_DIM)
                             for nm in ("ka", "va", "kb", "vb")]
    mk_p = memkv[0].reshape(batch, n_mem, MEM_HEADS, MEM_HEAD_DIM)
    mv_p = memkv[1].reshape(batch, n_mem, MEM_HEADS, MEM_HEAD_DIM)
    return (y_p.reshape(batch, seq, d_model), y_s.reshape(dec_batch, dec_seq, d_model),
            *kv(proj_p, batch, seq), mk_p, mv_p, *kv(proj_s, dec_batch, dec_seq))
```
